```python
import math
import jax, jax.numpy as jnp
from jax import lax
import numpy as np

D_MODEL = 1024
BATCH = 32
SEQ = 2048
DEPTH = 2

GRID_W = 64
CTX_LEN = 256

POOL_WIDTH = D_MODEL // 4
POOL_GROUPS = 4
POOL_WINDOWS = (2, 4, 8, 16)
HYENA_WIDTH = D_MODEL // 4
HYENA_SHORT = 3
HYENA_EMB = 33
HYENA_FILTER_ORDER = 64
HYENA_TARGET = 1e-2
HYENA_FAST_PCT = 0.3
HYENA_SLOW_PCT = 1.5
ATTN_WIDTH = D_MODEL // 2
HEAD_DIM = 64
ATTN_HEADS = ATTN_WIDTH // HEAD_DIM
ATTN_KV_HEADS = ATTN_HEADS // 4
ATTN_GROUP = ATTN_HEADS // ATTN_KV_HEADS
KV_WIDTH = ATTN_KV_HEADS * HEAD_DIM
Q_BLOCK = 128
ROPE_THETA = 10000.0
LN_EPS = 1e-6
QK_EPS = 1e-6
DEEPNORM_ALPHA = (2.0 * DEPTH) ** 0.25
DEEPNORM_BETA = (8.0 * DEPTH) ** -0.25

SPLIT_SIZES = (POOL_WIDTH, POOL_WIDTH, 3 * HYENA_WIDTH, HYENA_WIDTH, ATTN_WIDTH, KV_WIDTH, KV_WIDTH, ATTN_WIDTH)
IN_WIDTH = sum(SPLIT_SIZES)
SPLIT_POINTS = tuple(sum(SPLIT_SIZES[:i + 1]) for i in range(len(SPLIT_SIZES) - 1))
K_OFF = SPLIT_POINTS[4]
KV_END = SPLIT_POINTS[6]

kernel_name = "hybrid_pool_hyena_gqa_deepnorm_block"


def _layernorm(x):
    xf = x.astype(jnp.float32)
    mu = jnp.mean(xf, axis=-1, keepdims=True)
    var = jnp.mean(jnp.square(xf - mu), axis=-1, keepdims=True)
    return ((xf - mu) * lax.rsqrt(var + LN_EPS)).astype(x.dtype)


def _rms_head(x, g):
    xf = x.astype(jnp.float32)
    y = xf * lax.rsqrt(jnp.mean(xf * xf, axis=-1, keepdims=True) + QK_EPS) * g.astype(jnp.float32)
    return y.astype(x.dtype)


def _pool_mixer(v, pool_w, pool_scale):
    B, L, _ = v.shape
    gw = POOL_WIDTH // POOL_GROUPS
    vg = v.reshape(B, L, POOL_GROUPS, gw)
    cs = jnp.pad(jnp.cumsum(vg.astype(jnp.float32), axis=1), ((0, 0), (1, 0), (0, 0), (0, 0)))
    t = jnp.arange(L)
    means = []
    for g, w in enumerate(POOL_WINDOWS):
        lo = jnp.clip(t - w // 2, 0, L)
        hi = jnp.clip(t - w // 2 + w, 0, L)
        cnt = (hi - lo).astype(jnp.float32)[None, :, None]
        means.append((cs[:, hi, g] - cs[:, lo, g]) / cnt)
    mean = jnp.stack(means, axis=2).astype(v.dtype)
    y = jnp.einsum('blgc,gcd->blgd', mean - vg, pool_w).reshape(B, L, POOL_WIDTH)
    return y * pool_scale


def _short_conv(u, w, b):
    K, C = w.shape
    y = lax.conv_general_dilated(u, w[:, None, :].astype(u.dtype), window_strides=(1,),
                                 padding=[(K // 2, K // 2)], dimension_numbers=('NWC', 'WIO', 'NWC'),
                                 feature_group_count=C)
    return y + b


def _hyena_kernel(L, w1, b1, freq, w2, b2, w3):
    f32 = jnp.float32
    t = jnp.linspace(0.0, 1.0, L, dtype=f32)[:, None]
    bands = (HYENA_EMB - 1) // 2
    fr = jnp.linspace(1e-4, bands - 1, bands, dtype=f32)
    wpos = 2.0 * math.pi * jnp.arange(L, dtype=f32)[:, None] / L
    z = jnp.concatenate([t, jnp.cos(fr * wpos), -jnp.sin(fr * wpos)], axis=-1)
    fq = freq.astype(f32)
    h = jnp.sin(fq[0] * (z @ w1.astype(f32) + b1.astype(f32)))
    h = jnp.sin(fq[1] * (h @ w2.astype(f32) + b2.astype(f32)))
    h = (h @ w3.astype(f32)).reshape(L, 2, HYENA_WIDTH)
    max_decay = math.log(HYENA_TARGET) / HYENA_FAST_PCT
    min_decay = math.log(HYENA_TARGET) / HYENA_SLOW_PCT
    deltas = jnp.linspace(min_decay, max_decay, HYENA_WIDTH, dtype=f32)
    h = h * jnp.exp(-t * jnp.abs(deltas))[:, None, :]
    h_f, h_b = h[:, 0], h[:, 1]
    k = jnp.concatenate([h_f, jnp.zeros((1, HYENA_WIDTH), f32), h_b[:0:-1]], axis=0)
    return k / jnp.sum(jnp.abs(k), axis=0, keepdims=True)


def _bidir_fftconv(u, k, bias):
    L = u.shape[1]
    uf32 = u.astype(jnp.float32)
    uf = jnp.fft.rfft(uf32, n=2 * L, axis=1)
    kf = jnp.fft.rfft(k, n=2 * L, axis=0)
    y = jnp.fft.irfft(uf * kf[None], n=2 * L, axis=1)[:, :L]
    return (y + uf32 * bias.astype(jnp.float32)).astype(u.dtype)


def _hyena_mixer(uvx, conv_w, conv_b, w1, b1, freq, w2, b2, w3, bias):
    uvx = _short_conv(uvx, conv_w, conv_b)
    x0, x1, v = jnp.split(uvx, 3, axis=-1)
    k = _hyena_kernel(uvx.shape[1], w1, b1, freq, w2, b2, w3)
    return x0 * _bidir_fftconv(x1 * v, k, bias)


def _axial_rope(L):
    rows = L // GRID_W
    row = jnp.repeat(jnp.arange(rows), GRID_W)
    col = jnp.tile(jnp.arange(GRID_W), rows)
    axis_dim = HEAD_DIM // 2
    inv = ROPE_THETA ** (-jnp.arange(0, axis_dim, 2, dtype=jnp.float32) / axis_dim)
    ang = jnp.concatenate([row[:, None] * inv, col[:, None] * inv], axis=-1)
    return jnp.cos(ang), jnp.sin(ang)


def _apply_rope(x, cos, sin):
    B, L, H, D = x.shape
    xr = x.reshape(B, L, H, 2, 2, D // 4)
    a, b = xr[..., 0, :], xr[..., 1, :]
    c = cos.reshape(L, 1, 2, D // 4)
    s = sin.reshape(L, 1, 2, D // 4)
    return jnp.stack([a * c - b * s, a * s + b * c], axis=-2).reshape(B, L, H, D).astype(x.dtype)


def _sdpa(q, k, v):
    s = jnp.einsum('bqhgd,bkhd->bhgqk', q, k).astype(jnp.float32) * (HEAD_DIM ** -0.5)
    p = jax.nn.softmax(s, axis=-1).astype(v.dtype)
    return jnp.einsum('bhgqk,bkhd->bqhgd', p, v)


def _blocked_attention(q, k, v):
    B, L = q.shape[:2]
    nb = L // Q_BLOCK
    qb = jnp.moveaxis(q.reshape(B, nb, Q_BLOCK, *q.shape[2:]), 1, 0)
    ob = lax.map(lambda qq: _sdpa(qq, k, v), qb)
    return jnp.moveaxis(ob, 0, 1).reshape(q.shape)


def _heads_q(t, g, B, L):
    return _rms_head(t.reshape(B, L, ATTN_HEADS, HEAD_DIM), g)


def _heads_kv(t, B, L):
    return t.reshape(B, L, ATTN_KV_HEADS, HEAD_DIM)


def _merge(parts, attn_o, pool_w, pool_scale, hy_args, w_out):
    pool_v, pool_g, hy_u, hy_g, _, _, _, attn_g = parts
    y_pool = _pool_mixer(pool_v, pool_w, pool_scale) * jax.nn.silu(pool_g)
    y_hy = _hyena_mixer(hy_u, *hy_args) * jax.nn.silu(hy_g)
    y_attn = attn_o * jax.nn.silu(attn_g)
    return jnp.concatenate([y_pool, y_hy, y_attn], axis=-1) @ w_out


def setup_inputs(seed: int = 0) -> dict:
    key = jax.random.key(seed)
    ks = jax.random.split(key, 24)
    n = jax.random.normal
    f32 = jnp.float32
    D = D_MODEL
    return {
        "x": n(ks[0], (BATCH, SEQ, D), f32),
        "c": n(ks[1], (BATCH, D), f32),
        "ctx": n(ks[2], (BATCH, CTX_LEN, D), f32),
        "c_ctx": n(ks[3], (D,), f32),
        "w_ada": n(ks[4], (DEPTH, D, 3 * D), f32) * (0.5 * D ** -0.5),
        "b_ada": n(ks[5], (DEPTH, 3 * D), f32) * 0.01,
        "w_in": n(ks[6], (DEPTH, D, IN_WIDTH), f32) * D ** -0.5,
        "pool_w": n(ks[7], (DEPTH, POOL_GROUPS, POOL_WIDTH // POOL_GROUPS, POOL_WIDTH // POOL_GROUPS), f32) * (POOL_WIDTH // POOL_GROUPS) ** -0.5,
        "pool_scale": 1.0 + 0.02 * n(ks[8], (DEPTH, POOL_WIDTH), f32),
        "hy_conv_w": n(ks[9], (DEPTH, HYENA_SHORT, 3 * HYENA_WIDTH), f32) * HYENA_SHORT ** -0.5,
        "hy_conv_b": n(ks[10], (DEPTH, 3 * HYENA_WIDTH), f32) * 0.01,
        "hy_w1": n(ks[11], (DEPTH, HYENA_EMB, HYENA_FILTER_ORDER), f32) * HYENA_EMB ** -0.5,
        "hy_b1": n(ks[12], (DEPTH, HYENA_FILTER_ORDER), f32) * 0.01,
        "hy_freq": 1.0 + 0.02 * n(ks[13], (DEPTH, 2, HYENA_FILTER_ORDER), f32),
        "hy_w2": n(ks[14], (DEPTH, HYENA_FILTER_ORDER, HYENA_FILTER_ORDER), f32) * HYENA_FILTER_ORDER ** -0.5,
        "hy_b2": n(ks[15], (DEPTH, HYENA_FILTER_ORDER), f32) * 0.01,
        "hy_w3": n(ks[16], (DEPTH, HYENA_FILTER_ORDER, 2 * HYENA_WIDTH), f32) * HYENA_FILTER_ORDER ** -0.5,
        "hy_bias": n(ks[17], (DEPTH, HYENA_WIDTH), f32),
        "q_norm": 1.0 + 0.02 * n(ks[18], (DEPTH, HEAD_DIM), f32),
        "k_norm": 1.0 + 0.02 * n(ks[19], (DEPTH, HEAD_DIM), f32),
        "w_out": n(ks[20], (DEPTH, D, D), f32) * (D ** -0.5 * DEEPNORM_BETA),
        "ln_g": 1.0 + 0.02 * n(ks[21], (DEPTH, D), f32),
        "ln_b": n(ks[22], (DEPTH, D), f32) * 0.01,
    }


def reference(x, c, ctx, c_ctx, w_ada, b_ada, w_in, pool_w, pool_scale, hy_conv_w, hy_conv_b,
              hy_w1, hy_b1, hy_freq, hy_w2, hy_b2, hy_w3, hy_bias, q_norm, k_norm, w_out, ln_g, ln_b):
    B, L, _ = x.shape
    Lc = ctx.shape[1]
    cos, sin = _axial_rope(L)
    for i in range(DEPTH):
        last = i == DEPTH - 1
        mx = jax.nn.silu(c) @ w_ada[i] + b_ada[i]
        shift_x, scale_x, gate_x = jnp.split(mx[:, None, :], 3, axis=-1)
        mc = jax.nn.silu(c_ctx) @ w_ada[i] + b_ada[i]
        shift_c, scale_c, gate_c = jnp.split(mc, 3, axis=-1)
        hx = _layernorm(x) * (1.0 + scale_x) + shift_x
        hc = _layernorm(ctx) * (1.0 + scale_c) + shift_c
        hy_args = (hy_conv_w[i], hy_conv_b[i], hy_w1[i], hy_b1[i], hy_freq[i], hy_w2[i], hy_b2[i], hy_w3[i], hy_bias[i])

        if last:
            kc_raw, vc_raw = jnp.split(hc @ w_in[i][:, K_OFF:KV_END], [KV_WIDTH], axis=-1)
            pc = None
        else:
            pc = jnp.split(hc @ w_in[i], SPLIT_POINTS, axis=-1)
            kc_raw, vc_raw = pc[5], pc[6]
        kc = _rms_head(_heads_kv(kc_raw, B, Lc), k_norm[i])
        vc = _heads_kv(vc_raw, B, Lc)

        px = jnp.split(hx @ w_in[i], SPLIT_POINTS, axis=-1)
        qx = _apply_rope(_heads_q(px[4], q_norm[i], B, L), cos, sin).reshape(B, L, ATTN_KV_HEADS, ATTN_GROUP, HEAD_DIM)
        kx = _apply_rope(_rms_head(_heads_kv(px[5], B, L), k_norm[i]), cos, sin)
        vx = _heads_kv(px[6], B, L)
        k_all = jnp.concatenate([kx, kc], axis=1)
        v_all = jnp.concatenate([vx, vc], axis=1)
        ox = _blocked_attention(qx, k_all, v_all).reshape(B, L, ATTN_WIDTH)
        yx = _merge(px, ox, pool_w[i], pool_scale[i], hy_args, w_out[i])
        x_new = _layernorm(DEEPNORM_ALPHA * x + gate_x * yx) * ln_g[i] + ln_b[i]

        if not last:
            qc = _heads_q(pc[4], q_norm[i], B, Lc).reshape(B, Lc, ATTN_KV_HEADS, ATTN_GROUP, HEAD_DIM)
            oc = _sdpa(qc, kc, vc).reshape(B, Lc, ATTN_WIDTH)
            yc = _merge(pc, oc, pool_w[i], pool_scale[i], hy_args, w_out[i])
            ctx = _layernorm(DEEPNORM_ALPHA * ctx + gate_c * yc) * ln_g[i] + ln_b[i]
        x = x_new
    return x
```

```python
import functools
import math

import jax
import jax.numpy as jnp
from jax import lax
from jax.experimental import pallas as pl
from jax.experimental.pallas import tpu as pltpu

F32 = jnp.float32
BF16 = jnp.bfloat16
HIGHEST = lax.Precision.HIGHEST

GRID_W = 64
POOL_WIDTH = 256
POOL_GROUPS = 4
POOL_HALF_WINDOWS = (1, 2, 4, 8)
HYENA_WIDTH = 256
HYENA_EMB = 33
HYENA_EMB_PAD = 40
HYENA_TARGET = 1e-2
HYENA_FAST_PCT = 0.3
HYENA_SLOW_PCT = 1.5
ATTN_WIDTH = 512
HEAD_DIM = 64
ATTN_HEADS = 8
ATTN_GROUP = 4
KV_WIDTH = 128
ROPE_THETA = 10000.0
LN_EPS = 1e-6
QK_EPS = 1e-6

OFF_POOL_V, OFF_POOL_G, OFF_HY_U, OFF_HY_G, OFF_Q, OFF_K, OFF_V, OFF_ATTN_G, IN_WIDTH = (
    0, 256, 512, 1280, 1536, 2048, 2176, 2304, 2816)

LANES = 128
MOD_ROWS = 40
ROW_TILE = 512
Q_TILE = 256
VMEM_LIMIT = 56 * 1024 * 1024


def _silu(x):
    return x * jax.nn.sigmoid(x)


def _layernorm(x):
    mu = jnp.mean(x, axis=-1, keepdims=True)
    xc = x - mu
    var = jnp.mean(xc * xc, axis=-1, keepdims=True)
    return xc * lax.rsqrt(var + LN_EPS)


def _params(*sem):
    return pltpu.CompilerParams(dimension_semantics=sem, vmem_limit_bytes=VMEM_LIMIT)


def _whole(shape):
    nd = len(shape)
    return pl.BlockSpec(shape, lambda *_: (0,) * nd, pipeline_mode=pl.Buffered(1))


def _ada_kernel(c_ref, w_ref, b_ref, o_ref):
    s = _silu(c_ref[...])
    o_ref[0] = jnp.dot(s, w_ref[0], preferred_element_type=F32, precision=HIGHEST) + b_ref[0]


def _ada_modulation(cc, w_ada, b_ada):
    depth, d, d3 = w_ada.shape
    tn = 512
    return pl.pallas_call(
        _ada_kernel,
        grid=(depth, d3 // tn),
        in_specs=[
            pl.BlockSpec((MOD_ROWS, d), lambda i, j: (0, 0)),
            pl.BlockSpec((1, d, tn), lambda i, j: (i, 0, j)),
            pl.BlockSpec((1, 1, tn), lambda i, j: (i, 0, j)),
        ],
        out_specs=pl.BlockSpec((1, MOD_ROWS, tn), lambda i, j: (i, 0, j)),
        out_shape=jax.ShapeDtypeStruct((depth, MOD_ROWS, d3), F32),
        compiler_params=_params("parallel", "parallel"),
        name="ada_modulation",
    )(cc, w_ada, b_ada.reshape(depth, 1, d3))


def _head_rms(t, ones_bd, gain):
    t2 = t * t
    hi = t2.astype(BF16)
    lo = (t2 - hi.astype(F32)).astype(BF16)
    ss = (jnp.dot(hi, ones_bd, preferred_element_type=F32)
          + jnp.dot(lo, ones_bd, preferred_element_type=F32))
    return t * lax.rsqrt(ss * (1.0 / HEAD_DIM) + QK_EPS) * gain


def _rope(t, cos, sin):
    lane = lax.broadcasted_iota(jnp.int32, t.shape, 1)
    partner = jnp.where((lane & 16) == 0, pltpu.roll(t, LANES - 16, 1), pltpu.roll(t, 16, 1))
    return t * cos + partner * sin


def _inproj_kernel(*refs, rope, kv_only):
    x_ref, shift_ref, scale_ref, w_ref, qg_ref, kg_ref, ones_ref = refs[:7]
    refs = refs[7:]
    if rope:
        cos_ref, sin_ref = refs[:2]
        refs = refs[2:]
    h = _layernorm(x_ref[0]) * (1.0 + scale_ref[0]) + shift_ref[0]
    hb = h.astype(BF16)

    def proj(lo, hi):
        return jnp.dot(hb, w_ref[:, lo:hi], preferred_element_type=F32)

    if kv_only:
        kt_ref, v_ref = refs
    else:
        pv_ref, pg_ref, hu_ref, hg_ref, q_ref, kt_ref, v_ref, ag_ref = refs
        pv_ref[0] = proj(OFF_POOL_V, OFF_POOL_G)
        pg_ref[0] = _silu(proj(OFF_POOL_G, OFF_HY_U)).astype(BF16)
        hu_ref[0] = proj(OFF_HY_U, OFF_HY_G).astype(BF16)
        hg_ref[0] = _silu(proj(OFF_HY_G, OFF_Q)).astype(BF16)
        ag_ref[0] = _silu(proj(OFF_ATTN_G, IN_WIDTH)).astype(BF16)
        q = _head_rms(proj(OFF_Q, OFF_K), ones_ref[...], qg_ref[...] * HEAD_DIM ** -0.5)
        for j in range(ATTN_WIDTH // LANES):
            qj = q[:, j * LANES:(j + 1) * LANES]
            if rope:
                qj = _rope(qj, cos_ref[...], sin_ref[...])
            q_ref[0, :, j * LANES:(j + 1) * LANES] = qj.astype(BF16)
    k = _head_rms(proj(OFF_K, OFF_V), ones_ref[:KV_WIDTH, :KV_WIDTH], kg_ref[...])
    if rope:
        k = _rope(k, cos_ref[...], sin_ref[...])
    kt_ref[0] = k.T.astype(BF16)
    v_ref[0] = proj(OFF_V, OFF_ATTN_G).astype(BF16)


def _in_projection(x, mod, mod_row, w_in, q_gain, k_gain, ones_bd, rope_tabs, kv_only=False):
    b, l, d = x.shape
    tm = min(ROW_TILE, l)
    rope = rope_tabs is not None
    tok = lambda width: pl.BlockSpec((1, tm, width), lambda i, j: (i, j, 0))
    in_specs = [
        tok(d),
        pl.BlockSpec((1, 1, d), lambda i, j: (mod_row(i), 0, 0)),
        pl.BlockSpec((1, 1, d), lambda i, j: (mod_row(i), 0, 1)),
        _whole(w_in.shape), _whole(q_gain.shape), _whole(k_gain.shape), _whole(ones_bd.shape),
    ]
    args = [x, mod, mod, w_in, q_gain, k_gain, ones_bd]
    if rope:
        in_specs += [pl.BlockSpec((tm, LANES), lambda i, j: (j, 0))] * 2
        args += list(rope_tabs)
    kt_spec = pl.BlockSpec((1, KV_WIDTH, tm), lambda i, j: (i, 0, j))
    kv_shapes = [jax.ShapeDtypeStruct((b, KV_WIDTH, l), BF16), jax.ShapeDtypeStruct((b, l, KV_WIDTH), BF16)]
    if kv_only:
        out_specs = [kt_spec, tok(KV_WIDTH)]
        out_shape = kv_shapes
    else:
        widths = (POOL_WIDTH, POOL_WIDTH, 3 * HYENA_WIDTH, HYENA_WIDTH, ATTN_WIDTH)
        dtypes = (F32, BF16, BF16, BF16, BF16)
        out_specs = [tok(w) for w in widths] + [kt_spec, tok(KV_WIDTH), tok(ATTN_WIDTH)]
        out_shape = ([jax.ShapeDtypeStruct((b, l, w), t) for w, t in zip(widths, dtypes)]
                     + kv_shapes + [jax.ShapeDtypeStruct((b, l, ATTN_WIDTH), BF16)])
    return pl.pallas_call(
        functools.partial(_inproj_kernel, rope=rope, kv_only=kv_only),
        grid=(b, l // tm),
        in_specs=in_specs,
        out_specs=out_specs,
        out_shape=out_shape,
        compiler_params=_params("parallel", "arbitrary"),
        name="in_projection",
    )(*args)


def _attn_kernel(*refs, n_src):
    q_ref, ag_ref = refs[:2]
    kt_refs = refs[2:2 + 2 * n_src:2]
    v_refs = refs[3:3 + 2 * n_src:2]
    o_ref = refs[2 + 2 * n_src]
    tq = q_ref.shape[1]
    lane = lax.broadcasted_iota(jnp.int32, (tq, LANES), 1)
    for pair in range(ATTN_HEADS // 2):
        halves = []
        q_pair = q_ref[0, :, pair * LANES:(pair + 1) * LANES]
        for sub in range(2):
            hq = 2 * pair + sub
            kvh = hq // ATTN_GROUP
            qh = q_pair[:, sub * HEAD_DIM:(sub + 1) * HEAD_DIM]
            scores = [jnp.dot(qh, kt[0, kvh * HEAD_DIM:(kvh + 1) * HEAD_DIM, :], preferred_element_type=F32)
                      for kt in kt_refs]
            m = functools.reduce(jnp.maximum, [jnp.max(s, axis=-1, keepdims=True) for s in scores])
            probs = [jnp.exp(s - m) for s in scores]
            denom = sum(jnp.sum(p, axis=-1, keepdims=True) for p in probs)
            o = sum(jnp.dot(p.astype(BF16), v[0], preferred_element_type=F32) for p, v in zip(probs, v_refs))
            o = o / denom
            if kvh != sub:
                o = pltpu.roll(o, HEAD_DIM, 1)
            halves.append(o)
        col = jnp.where(lane < HEAD_DIM, halves[0], halves[1])
        gate = ag_ref[0, :, pair * LANES:(pair + 1) * LANES].astype(F32)
        o_ref[0, :, pair * LANES:(pair + 1) * LANES] = (col * gate).astype(BF16)


def _attention(q, ag, kv_sources):
    b, l, _ = q.shape
    tq = min(Q_TILE, l)
    tok = pl.BlockSpec((1, tq, ATTN_WIDTH), lambda i, j: (i, j, 0))
    in_specs = [tok, tok]
    args = [q, ag]
    for kt, v in kv_sources:
        lk = v.shape[1]
        in_specs += [pl.BlockSpec((1, KV_WIDTH, lk), lambda i, j: (i, 0, 0)),
                     pl.BlockSpec((1, lk, KV_WIDTH), lambda i, j: (i, 0, 0))]
        args += [kt, v]
    return pl.pallas_call(
        functools.partial(_attn_kernel, n_src=len(kv_sources)),
        grid=(b, l // tq),
        in_specs=in_specs,
        out_specs=tok,
        out_shape=jax.ShapeDtypeStruct((b, l, ATTN_WIDTH), BF16),
        compiler_params=_params("parallel", "arbitrary"),
        name="attention",
    )(*args)


def _seq_local_kernel(pv_ref, pg_ref, hu_ref, hg_ref, cw_ref, cb_ref, pbd_ref, psc_ref,
                      yp_ref, u_ref, m0_ref):
    l = pv_ref.shape[1]
    row = lax.broadcasted_iota(jnp.int32, (l, POOL_WIDTH), 0)
    lane = lax.broadcasted_iota(jnp.int32, (l, POOL_WIDTH), 1)

    def down(x, j):
        return jnp.where(row >= j, pltpu.roll(x, j, 0), 0.0)

    def up(x, j):
        return jnp.where(row < l - j, pltpu.roll(x, l - j, 0), 0.0)

    v = pv_ref[0]
    back, fwd = v + down(v, 1), v + up(v, 1)
    windows = [back]
    for half in POOL_HALF_WINDOWS[1:]:
        windows.append(down(back, 1) + fwd)
        back, fwd = back + down(back, half), fwd + up(fwd, half)
    group = lane // (POOL_WIDTH // POOL_GROUPS)
    wsum = windows[-1]
    half_w = jnp.full_like(row, POOL_HALF_WINDOWS[-1])
    for g in range(POOL_GROUPS - 2, -1, -1):
        wsum = jnp.where(group == g, windows[g], wsum)
        half_w = jnp.where(group == g, POOL_HALF_WINDOWS[g], half_w)
    cnt = jnp.minimum(row + half_w, l) - jnp.maximum(row - half_w, 0)
    diff = wsum / cnt.astype(F32) - v
    y = jnp.dot(diff.astype(BF16), pbd_ref[...], preferred_element_type=F32)
    yp_ref[0] = (y * psc_ref[...] * pg_ref[0].astype(F32)).astype(BF16)

    parts = []
    for p in range(3):
        cols = slice(p * HYENA_WIDTH, (p + 1) * HYENA_WIDTH)
        x = hu_ref[0, :, cols].astype(F32)
        parts.append(cw_ref[0:1, cols] * down(x, 1) + cw_ref[1:2, cols] * x
                     + cw_ref[2:3, cols] * up(x, 1) + cb_ref[:, cols])
    u_ref[0] = parts[1] * parts[2]
    m0_ref[0] = (parts[0] * hg_ref[0].astype(F32)).astype(BF16)


def _seq_local(pv, pg, hu, hg, conv_w, conv_b, pool_bd, pool_scale):
    b, l, _ = pv.shape
    tok = lambda width: pl.BlockSpec((1, l, width), lambda i: (i, 0, 0))
    return pl.pallas_call(
        _seq_local_kernel,
        grid=(b,),
        in_specs=[tok(POOL_WIDTH), tok(POOL_WIDTH), tok(3 * HYENA_WIDTH), tok(HYENA_WIDTH),
                  _whole(conv_w.shape), _whole(conv_b.shape), _whole(pool_bd.shape), _whole(pool_scale.shape)],
        out_specs=[tok(POOL_WIDTH), tok(HYENA_WIDTH), tok(HYENA_WIDTH)],
        out_shape=[jax.ShapeDtypeStruct((b, l, POOL_WIDTH), BF16),
                   jax.ShapeDtypeStruct((b, l, HYENA_WIDTH), F32),
                   jax.ShapeDtypeStruct((b, l, HYENA_WIDTH), BF16)],
        compiler_params=_params("parallel"),
        name="seq_local",
    )(pv, pg, hu, hg, conv_w, conv_b, pool_bd, pool_scale)


def _filter_kernel(z_ref, t_ref, delta_ref, w1_ref, b1_ref, fq_ref, w2_ref, b2_ref, w3_ref,
                   cp_ref, sp_ref, kre_ref, kim_ref, ks_ref, kd_ref):
    l = z_ref.shape[0]

    @pl.when(pl.program_id(0) == 0)
    def _():
        dot = functools.partial(jnp.dot, preferred_element_type=F32, precision=HIGHEST)
        h = jnp.sin(fq_ref[0:1, :] * (dot(z_ref[...], w1_ref[...]) + b1_ref[...]))
        h = jnp.sin(fq_ref[1:2, :] * (dot(h, w2_ref[...]) + b2_ref[...]))
        h = dot(h, w3_ref[...])
        decay = jnp.exp(-t_ref[...] * jnp.abs(delta_ref[...]))
        h_f = h[:, :HYENA_WIDTH] * decay
        row = lax.broadcasted_iota(jnp.int32, (l, HYENA_WIDTH), 0)
        h_b = jnp.where(row == 0, 0.0, h[:, HYENA_WIDTH:] * decay)
        norm = jnp.sum(jnp.abs(h_f) + jnp.abs(h_b), axis=0, keepdims=True)
        ks_ref[...] = (h_f + h_b) / norm
        kd_ref[...] = (h_f - h_b) / norm

    kre_ref[...] = jnp.dot(cp_ref[...], ks_ref[...], preferred_element_type=F32, precision=HIGHEST) * (1.0 / l)
    kim_ref[...] = jnp.dot(sp_ref[...], kd_ref[...], preferred_element_type=F32, precision=HIGHEST) * (-1.0 / l)


def _filter_spectrum(z, t, deltas, w1, b1, freq, w2, b2, w3, cp, sp):
    l = z.shape[0]
    fb = min(256, l)
    blk = pl.BlockSpec((fb, l), lambda i: (i, 0))
    out = pl.BlockSpec((fb, HYENA_WIDTH), lambda i: (i, 0))
    return pl.pallas_call(
        _filter_kernel,
        grid=(l // fb,),
        in_specs=[_whole(a.shape) for a in (z, t, deltas, w1, b1, freq, w2, b2, w3)] + [blk, blk],
        out_specs=[out, out],
        out_shape=[jax.ShapeDtypeStruct((l, HYENA_WIDTH), F32)] * 2,
        scratch_shapes=[pltpu.VMEM((l, HYENA_WIDTH), F32)] * 2,
        compiler_params=_params("arbitrary"),
        name="filter_spectrum",
    )(z, t, deltas, w1, b1, freq, w2, b2, w3, cp, sp)


def _long_conv_kernel(u_ref, m0_ref, cs_ref, ss_ref, kre_ref, kim_ref, bias_ref, o_ref):
    u = u_ref[0]
    ub = u.astype(BF16)
    a = jnp.dot(cs_ref[...], ub, preferred_element_type=F32)
    b = jnp.dot(ss_ref[...], ub, preferred_element_type=F32)
    kre, kim = kre_ref[...], kim_ref[...]
    zre = (a * kre + b * kim).astype(BF16)
    zim = (a * kim - b * kre).astype(BF16)
    y = (jnp.dot(cs_ref[...], zre, preferred_element_type=F32)
         - jnp.dot(ss_ref[...], zim, preferred_element_type=F32))
    o_ref[0] = ((y + u * bias_ref[...]) * m0_ref[0].astype(F32)).astype(BF16)


def _long_conv(u, m0, cs, ss, kre, kim, bias):
    b, l, _ = u.shape
    tok = pl.BlockSpec((1, l, HYENA_WIDTH), lambda i: (i, 0, 0))
    return pl.pallas_call(
        _long_conv_kernel,
        grid=(b,),
        in_specs=[tok, tok] + [_whole(a.shape) for a in (cs, ss, kre, kim, bias)],
        out_specs=tok,
        out_shape=jax.ShapeDtypeStruct((b, l, HYENA_WIDTH), BF16),
        compiler_params=_params("parallel"),
        name="long_conv",
    )(u, m0, cs, ss, kre, kim, bias)


def _outproj_kernel(yp_ref, yh_ref, ya_ref, x_ref, gate_ref, w_ref, g_ref, b_ref, o_ref, *, alpha):
    y = (jnp.dot(yp_ref[0], w_ref[0:POOL_WIDTH, :], preferred_element_type=F32)
         + jnp.dot(yh_ref[0], w_ref[POOL_WIDTH:POOL_WIDTH + HYENA_WIDTH, :], preferred_element_type=F32)
         + jnp.dot(ya_ref[0], w_ref[POOL_WIDTH + HYENA_WIDTH:, :], preferred_element_type=F32))
    o_ref[0] = _layernorm(alpha * x_ref[0] + gate_ref[0] * y) * g_ref[...] + b_ref[...]


def _out_projection(yp, yh, ya, x, mod, mod_row, w_out, ln_g, ln_b, alpha):
    b, l, d = x.shape
    tm = min(ROW_TILE, l)
    tok = lambda width: pl.BlockSpec((1, tm, width), lambda i, j: (i, j, 0))
    return pl.pallas_call(
        functools.partial(_outproj_kernel, alpha=alpha),
        grid=(b, l // tm),
        in_specs=[tok(POOL_WIDTH), tok(HYENA_WIDTH), tok(ATTN_WIDTH), tok(d),
                  pl.BlockSpec((1, 1, d), lambda i, j: (mod_row(i), 0, 2)),
                  _whole(w_out.shape), _whole(ln_g.shape), _whole(ln_b.shape)],
        out_specs=tok(d),
        out_shape=jax.ShapeDtypeStruct((b, l, d), F32),
        compiler_params=_params("parallel", "arbitrary"),
        name="out_projection",
    )(yp, yh, ya, x, mod, w_out, ln_g, ln_b)


def _rope_tables(l):
    rows = l // GRID_W
    row = jnp.repeat(jnp.arange(rows), GRID_W)
    col = jnp.tile(jnp.arange(GRID_W), rows)
    axis_dim = HEAD_DIM // 2
    inv = ROPE_THETA ** (-jnp.arange(0, axis_dim, 2, dtype=F32) / axis_dim)
    a0, a1 = row[:, None] * inv, col[:, None] * inv
    cos = jnp.concatenate([jnp.cos(a0)] * 2 + [jnp.cos(a1)] * 2, axis=-1)
    sin = jnp.concatenate([-jnp.sin(a0), jnp.sin(a0), -jnp.sin(a1), jnp.sin(a1)], axis=-1)
    reps = LANES // HEAD_DIM
    return jnp.tile(cos, (1, reps)), jnp.tile(sin, (1, reps))


def _trig_tables(l):
    i = jnp.arange(l, dtype=jnp.int32)
    odd = 2 * i + 1
    m_sym = (odd[:, None] * odd[None, :]) % (8 * l)
    ang = m_sym.astype(F32) * (2.0 * math.pi / (8 * l))
    m_plain = (odd[:, None] * i[None, :]) % (4 * l)
    angp = m_plain.astype(F32) * (2.0 * math.pi / (4 * l))
    return jnp.cos(ang).astype(BF16), jnp.sin(ang).astype(BF16), jnp.cos(angp), jnp.sin(angp)


def _filter_features(l):
    t = jnp.linspace(0.0, 1.0, l, dtype=F32)[:, None]
    bands = (HYENA_EMB - 1) // 2
    fr = jnp.linspace(1e-4, bands - 1, bands, dtype=F32)
    wpos = 2.0 * math.pi * jnp.arange(l, dtype=F32)[:, None] / l
    z = jnp.concatenate([t, jnp.cos(fr * wpos), -jnp.sin(fr * wpos)], axis=-1)
    z = jnp.pad(z, ((0, 0), (0, HYENA_EMB_PAD - HYENA_EMB)))
    max_decay = math.log(HYENA_TARGET) / HYENA_FAST_PCT
    min_decay = math.log(HYENA_TARGET) / HYENA_SLOW_PCT
    deltas = jnp.linspace(min_decay, max_decay, HYENA_WIDTH, dtype=F32)[None, :]
    return z, t, deltas


def kernel(x, c, ctx, c_ctx, w_ada, b_ada, w_in, pool_w, pool_scale, hy_conv_w, hy_conv_b, hy_w1, hy_b1,
           hy_freq, hy_w2, hy_b2, hy_w3, hy_bias, q_norm, k_norm, w_out, ln_g, ln_b):
    b, l, d = x.shape
    lc = ctx.shape[1]
    depth = w_in.shape[0]
    assert b + 1 <= MOD_ROWS and d == 1024 and w_in.shape[2] == IN_WIDTH
    alpha = (2.0 * depth) ** 0.25

    cc = jnp.concatenate([c, c_ctx[None, :], jnp.zeros((MOD_ROWS - b - 1, d), F32)], axis=0)
    mod_all = _ada_modulation(cc, w_ada, b_ada).reshape(depth, MOD_ROWS, 1, 3 * d)
    lat_row = lambda i: i
    ctx_row = lambda i: b

    head = jnp.arange(ATTN_WIDTH) // HEAD_DIM
    ones_bd = (head[:, None] == head[None, :]).astype(BF16)
    rope_tabs = _rope_tables(l)
    tables = {n: (_trig_tables(n), _filter_features(n)) for n in {l, lc}}

    for i in range(depth):
        last = i == depth - 1
        mod = mod_all[i]
        w_in_b = w_in[i].astype(BF16)
        w_out_b = w_out[i].astype(BF16)
        q_gain = jnp.tile(q_norm[i], ATTN_HEADS)[None, :]
        k_gain = jnp.tile(k_norm[i], KV_WIDTH // HEAD_DIM)[None, :]
        pool_bd = jax.scipy.linalg.block_diag(*pool_w[i]).astype(BF16)
        w1_pad = jnp.pad(hy_w1[i], ((0, HYENA_EMB_PAD - HYENA_EMB), (0, 0)))

        def mixers(pv, pg, hu, hg, attn_o, n):
            (cs, ss, cp, sp), (z, t, deltas) = tables[n]
            kre, kim = _filter_spectrum(z, t, deltas, w1_pad, hy_b1[i][None, :], hy_freq[i], hy_w2[i],
                                        hy_b2[i][None, :], hy_w3[i], cp, sp)
            yp, u, m0 = _seq_local(pv, pg, hu, hg, hy_conv_w[i], hy_conv_b[i][None, :], pool_bd,
                                   pool_scale[i][None, :])
            yh = _long_conv(u, m0, cs, ss, kre, kim, hy_bias[i][None, :])
            return yp, yh, attn_o

        if last:
            ktc, vc = _in_projection(ctx, mod, ctx_row, w_in_b, q_gain, k_gain, ones_bd, None, kv_only=True)
        else:
            pvc, pgc, huc, hgc, qc, ktc, vc, agc = _in_projection(
                ctx, mod, ctx_row, w_in_b, q_gain, k_gain, ones_bd, None)

        pv, pg, hu, hg, q, kt, v, ag = _in_projection(x, mod, lat_row, w_in_b, q_gain, k_gain, ones_bd, rope_tabs)
        attn_o = _attention(q, ag, [(kt, v), (ktc, vc)])
        x_new = _out_projection(*mixers(pv, pg, hu, hg, attn_o, l), x, mod, lat_row, w_out_b,
                                ln_g[i][None, :], ln_b[i][None, :], alpha)
        if not last:
            attn_c = _attention(qc, agc, [(ktc, vc)])
            ctx = _out_projection(*mixers(pvc, pgc, huc, hgc, attn_c, lc), ctx, mod, ctx_row, w_out_b,
                                  ln_g[i][None, :], ln_b[i][None, :], alpha)
        x = x_new
    return x
```

```python
import functools
import math

import jax
import jax.numpy as jnp
from jax import lax
from jax.experimental import pallas as pl
from jax.experimental.pallas import tpu as pltpu

F32 = jnp.float32
BF16 = jnp.bfloat16
HIGHEST = lax.Precision.HIGHEST

GRID_W = 64
POOL_WIDTH = 256
POOL_GROUPS = 4
POOL_HALF_WINDOWS = (1, 2, 4, 8)
HYENA_WIDTH = 256
HYENA_EMB = 33
HYENA_EMB_PAD = 40
HYENA_TARGET = 1e-2
HYENA_FAST_PCT = 0.3
HYENA_SLOW_PCT = 1.5
ATTN_WIDTH = 512
HEAD_DIM = 64
ATTN_HEADS = 8
ATTN_GROUP = 4
KV_WIDTH = 128
ROPE_THETA = 10000.0
LN_EPS = 1e-6
QK_EPS = 1e-6
LOG2_E = math.log2(math.e)

OFF_POOL_V, OFF_POOL_G, OFF_HY_U, OFF_HY_G, OFF_Q, OFF_K, OFF_V, OFF_ATTN_G, IN_WIDTH = (
    0, 256, 512, 1280, 1536, 2048, 2176, 2304, 2816)

LANES = 128
MOD_ROWS = 40
ROW_TILE = 512
HALO = 8
Q_TILE = 512
VMEM_LIMIT = 56 * 1024 * 1024


def _silu(x):
    return x * jax.nn.sigmoid(x)


def _layernorm(x):
    mu = jnp.mean(x, axis=-1, keepdims=True)
    xc = x - mu
    var = jnp.mean(xc * xc, axis=-1, keepdims=True)
    return xc * lax.rsqrt(var + LN_EPS)


def _params(*sem):
    return pltpu.CompilerParams(dimension_semantics=sem, vmem_limit_bytes=VMEM_LIMIT)


def _whole(shape):
    nd = len(shape)
    return pl.BlockSpec(shape, lambda *_: (0,) * nd, pipeline_mode=pl.Buffered(1))


def _ada_kernel(c_ref, w_ref, b_ref, o_ref):
    s = _silu(c_ref[...])
    o_ref[0] = jnp.dot(s, w_ref[0], preferred_element_type=F32, precision=HIGHEST) + b_ref[0]


def _ada_modulation(cc, w_ada, b_ada):
    depth, d, d3 = w_ada.shape
    tn = 512
    return pl.pallas_call(
        _ada_kernel,
        grid=(depth, d3 // tn),
        in_specs=[
            pl.BlockSpec((MOD_ROWS, d), lambda i, j: (0, 0)),
            pl.BlockSpec((1, d, tn), lambda i, j: (i, 0, j)),
            pl.BlockSpec((1, 1, tn), lambda i, j: (i, 0, j)),
        ],
        out_specs=pl.BlockSpec((1, MOD_ROWS, tn), lambda i, j: (i, 0, j)),
        out_shape=jax.ShapeDtypeStruct((depth, MOD_ROWS, d3), F32),
        compiler_params=_params("parallel", "parallel"),
        name="ada_modulation",
    )(cc, w_ada, b_ada.reshape(depth, 1, d3))


def _head_rms(t, ones_bd, gain):
    ss = jnp.dot((t * t).astype(BF16), ones_bd, preferred_element_type=F32)
    return t * lax.rsqrt(ss * (1.0 / HEAD_DIM) + QK_EPS) * gain


def _rope(t, cos, sin):
    lane = lax.broadcasted_iota(jnp.int32, t.shape, 1)
    partner = jnp.where((lane & 16) == 0, pltpu.roll(t, LANES - 16, 1), pltpu.roll(t, 16, 1))
    return t * cos + partner * sin


def _emit_kv(proj, rows, ones_ref, kg_ref, rope_refs, kt_ref, v_ref):
    k = _head_rms(proj(OFF_K, OFF_V), ones_ref[:KV_WIDTH, :KV_WIDTH], kg_ref[...])
    if rope_refs:
        k = _rope(k, rope_refs[0][...], rope_refs[1][...])
    kt_ref[0] = k.T.astype(BF16)
    v_ref[0, :, :KV_WIDTH] = proj(OFF_V, OFF_ATTN_G).astype(BF16)
    v_ref[0, :, KV_WIDTH:] = jnp.ones((rows, KV_WIDTH), BF16)


def _kv_proj_kernel(x_ref, shift_ref, scale_ref, w_ref, kg_ref, ones_ref, kt_ref, v_ref):
    hb = (_layernorm(x_ref[0]) * (1.0 + scale_ref[0]) + shift_ref[0]).astype(BF16)
    proj = lambda lo, hi: jnp.dot(hb, w_ref[:, lo:hi], preferred_element_type=F32)
    _emit_kv(proj, x_ref.shape[1], ones_ref, kg_ref, None, kt_ref, v_ref)


def _inproj_kernel(*refs, rope, seq_len):
    (x_ref, xp_ref, xn_ref, shift_ref, scale_ref, w_ref, qg_ref, kg_ref, ones_ref,
     cw_ref, cb_ref, pbd_ref, psc_ref) = refs[:13]
    rope_refs = refs[13:15] if rope else None
    yp_ref, u_ref, m0_ref, q_ref, kt_ref, v_ref, ag_ref = refs[15:] if rope else refs[13:]
    tm = x_ref.shape[1]
    te = tm + 2 * HALO

    xe = jnp.concatenate([xp_ref[0], x_ref[0], xn_ref[0]], axis=0)
    he = (_layernorm(xe) * (1.0 + scale_ref[0]) + shift_ref[0]).astype(BF16)
    hb = he[HALO:HALO + tm]
    proj = lambda lo, hi: jnp.dot(hb, w_ref[:, lo:hi], preferred_element_type=F32)
    proj_ext = lambda lo, hi: jnp.dot(he, w_ref[:, lo:hi], preferred_element_type=F32)

    pos = (pl.program_id(1) * tm - HALO) + lax.broadcasted_iota(jnp.int32, (te, LANES), 0)
    inside = (pos >= 0) & (pos < seq_len)
    first = lax.broadcasted_iota(jnp.int32, (te, LANES), 1) < LANES // 2
    up = lambda a, k: pltpu.roll(a, te - k, 0)
    down = lambda a, k: pltpu.roll(a, k, 0)
    core = lambda a: a[HALO:HALO + tm]

    pv = proj_ext(OFF_POOL_V, OFF_POOL_G)
    diffs = []
    for c in range(POOL_WIDTH // LANES):
        v = jnp.where(inside, pv[:, c * LANES:(c + 1) * LANES], 0.0)
        f2 = v + up(v, 1)
        f4 = f2 + up(f2, 2)
        if c == 0:
            wsum = jnp.where(first, down(f2, 1), down(f4, 2))
            half = jnp.where(first, POOL_HALF_WINDOWS[0], POOL_HALF_WINDOWS[1])
        else:
            f8 = f4 + up(f4, 4)
            wsum = jnp.where(first, down(f8, 4), down(f8, 8) + f8)
            half = jnp.where(first, POOL_HALF_WINDOWS[2], POOL_HALF_WINDOWS[3])
        cnt = jnp.maximum(jnp.minimum(pos + half, seq_len) - jnp.maximum(pos - half, 0), 1)
        diffs.append(core(wsum / cnt.astype(F32) - v).astype(BF16))
    y = jnp.dot(jnp.concatenate(diffs, axis=1), pbd_ref[...], preferred_element_type=F32)
    yp_ref[0] = (y * psc_ref[...] * _silu(proj(OFF_POOL_G, OFF_HY_U))).astype(BF16)

    hu = proj_ext(OFF_HY_U, OFF_HY_G)
    hgate = _silu(proj(OFF_HY_G, OFF_Q))
    ncol = HYENA_WIDTH // LANES
    conv = []
    for c in range(3 * ncol):
        cols = slice(c * LANES, (c + 1) * LANES)
        a = jnp.where(inside, hu[:, cols], 0.0)
        conv.append(core(cw_ref[0:1, cols] * down(a, 1) + cw_ref[1:2, cols] * a
                         + cw_ref[2:3, cols] * up(a, 1)) + cb_ref[:, cols])
    for c in range(ncol):
        cols = slice(c * LANES, (c + 1) * LANES)
        u_ref[0, :, cols] = conv[ncol + c] * conv[2 * ncol + c]
        m0_ref[0, :, cols] = (conv[c] * hgate[:, cols]).astype(BF16)

    ag_ref[0] = _silu(proj(OFF_ATTN_G, IN_WIDTH)).astype(BF16)
    q = _head_rms(proj(OFF_Q, OFF_K), ones_ref[...], qg_ref[...] * (HEAD_DIM ** -0.5 * LOG2_E))
    for c in range(ATTN_WIDTH // LANES):
        qc = q[:, c * LANES:(c + 1) * LANES]
        if rope:
            qc = _rope(qc, rope_refs[0][...], rope_refs[1][...])
        q_ref[0, :, c * LANES:(c + 1) * LANES] = qc.astype(BF16)
    _emit_kv(proj, tm, ones_ref, kg_ref, rope_refs, kt_ref, v_ref)


def _kv_specs(b, l, tm):
    tok = pl.BlockSpec((1, tm, 2 * KV_WIDTH), lambda i, j: (i, j, 0))
    kt_spec = pl.BlockSpec((1, KV_WIDTH, tm), lambda i, j: (i, 0, j))
    return ([kt_spec, tok],
            [jax.ShapeDtypeStruct((b, KV_WIDTH, l), BF16), jax.ShapeDtypeStruct((b, l, 2 * KV_WIDTH), BF16)])


def _mod_spec(d, mod_row, col):
    return pl.BlockSpec((1, 1, d), lambda i, j: (mod_row(i), 0, col))


def _kv_projection(x, mod, mod_row, w_in, k_gain, ones_bd):
    b, l, d = x.shape
    tm = min(ROW_TILE, l)
    out_specs, out_shape = _kv_specs(b, l, tm)
    return pl.pallas_call(
        _kv_proj_kernel,
        grid=(b, l // tm),
        in_specs=[pl.BlockSpec((1, tm, d), lambda i, j: (i, j, 0)), _mod_spec(d, mod_row, 0),
                  _mod_spec(d, mod_row, 1), _whole(w_in.shape), _whole(k_gain.shape), _whole(ones_bd.shape)],
        out_specs=out_specs,
        out_shape=out_shape,
        compiler_params=_params("parallel", "arbitrary"),
        name="kv_projection",
    )(x, mod, mod, w_in, k_gain, ones_bd)


def _in_projection(x, mod, mod_row, w_in, q_gain, k_gain, ones_bd, rope_tabs, conv_w, conv_b, pool_bd, pool_scale):
    b, l, d = x.shape
    tm = min(ROW_TILE, l)
    rope = rope_tabs is not None
    tok = lambda width: pl.BlockSpec((1, tm, width), lambda i, j: (i, j, 0))
    halo_blocks, last_halo = tm // HALO, l // HALO - 1
    consts = (w_in, q_gain, k_gain, ones_bd, conv_w, conv_b, pool_bd, pool_scale)
    in_specs = [
        tok(d),
        pl.BlockSpec((1, HALO, d), lambda i, j: (i, jnp.maximum(j * halo_blocks - 1, 0), 0)),
        pl.BlockSpec((1, HALO, d), lambda i, j: (i, jnp.minimum((j + 1) * halo_blocks, last_halo), 0)),
        _mod_spec(d, mod_row, 0), _mod_spec(d, mod_row, 1),
    ] + [_whole(a.shape) for a in consts]
    args = [x, x, x, mod, mod, *consts]
    if rope:
        in_specs += [pl.BlockSpec((tm, LANES), lambda i, j: (j, 0))] * 2
        args += list(rope_tabs)
    kv_specs, kv_shapes = _kv_specs(b, l, tm)
    widths = (POOL_WIDTH, HYENA_WIDTH, HYENA_WIDTH, ATTN_WIDTH)
    dtypes = (BF16, F32, BF16, BF16)
    return pl.pallas_call(
        functools.partial(_inproj_kernel, rope=rope, seq_len=l),
        grid=(b, l // tm),
        in_specs=in_specs,
        out_specs=[tok(w) for w in widths] + kv_specs + [tok(ATTN_WIDTH)],
        out_shape=([jax.ShapeDtypeStruct((b, l, w), t) for w, t in zip(widths, dtypes)]
                   + kv_shapes + [jax.ShapeDtypeStruct((b, l, ATTN_WIDTH), BF16)]),
        compiler_params=_params("parallel", "arbitrary"),
        name="in_projection",
    )(*args)


def _attn_kernel(*refs, n_src):
    q_ref, ag_ref = refs[:2]
    kt_refs = refs[2:2 + 2 * n_src:2]
    v_refs = refs[3:3 + 2 * n_src:2]
    o_ref = refs[2 + 2 * n_src]
    tq = q_ref.shape[1]
    lane = lax.broadcasted_iota(jnp.int32, (tq, LANES), 1)
    for pair in range(ATTN_HEADS // 2):
        halves = []
        q_pair = q_ref[0, :, pair * LANES:(pair + 1) * LANES]
        for sub in range(2):
            hq = 2 * pair + sub
            kvh = hq // ATTN_GROUP
            qh = q_pair[:, sub * HEAD_DIM:(sub + 1) * HEAD_DIM]
            scores = [jnp.dot(qh, kt[0, kvh * HEAD_DIM:(kvh + 1) * HEAD_DIM, :], preferred_element_type=F32)
                      for kt in kt_refs]
            m = functools.reduce(jnp.maximum, [jnp.max(s, axis=-1, keepdims=True) for s in scores])
            probs = [jnp.exp2(s - m).astype(BF16) for s in scores]
            o = sum(jnp.dot(p, v[0], preferred_element_type=F32) for p, v in zip(probs, v_refs))
            o = o[:, :LANES] / o[:, LANES:]
            if kvh != sub:
                o = pltpu.roll(o, HEAD_DIM, 1)
            halves.append(o)
        col = jnp.where(lane < HEAD_DIM, halves[0], halves[1])
        gate = ag_ref[0, :, pair * LANES:(pair + 1) * LANES].astype(F32)
        o_ref[0, :, pair * LANES:(pair + 1) * LANES] = (col * gate).astype(BF16)


def _attention(q, ag, kv_sources):
    b, l, _ = q.shape
    tq = min(Q_TILE, l)
    tok = pl.BlockSpec((1, tq, ATTN_WIDTH), lambda i, j: (i, j, 0))
    in_specs = [tok, tok]
    args = [q, ag]
    for kt, v in kv_sources:
        lk = v.shape[1]
        in_specs += [pl.BlockSpec((1, KV_WIDTH, lk), lambda i, j: (i, 0, 0)),
                     pl.BlockSpec((1, lk, 2 * KV_WIDTH), lambda i, j: (i, 0, 0))]
        args += [kt, v]
    return pl.pallas_call(
        functools.partial(_attn_kernel, n_src=len(kv_sources)),
        grid=(b, l // tq),
        in_specs=in_specs,
        out_specs=tok,
        out_shape=jax.ShapeDtypeStruct((b, l, ATTN_WIDTH), BF16),
        compiler_params=_params("parallel", "arbitrary"),
        name="attention",
    )(*args)


def _filter_kernel(z_ref, t_ref, delta_ref, w1_ref, b1_ref, fq_ref, w2_ref, b2_ref, w3_ref,
                   cp_ref, sp_ref, kre_ref, kim_ref, ks_ref, kd_ref):
    l = z_ref.shape[0]

    @pl.when(pl.program_id(0) == 0)
    def _():
        dot = functools.partial(jnp.dot, preferred_element_type=F32, precision=HIGHEST)
        h = jnp.sin(fq_ref[0:1, :] * (dot(z_ref[...], w1_ref[...]) + b1_ref[...]))
        h = jnp.sin(fq_ref[1:2, :] * (dot(h, w2_ref[...]) + b2_ref[...]))
        h = dot(h, w3_ref[...])
        decay = jnp.exp(-t_ref[...] * jnp.abs(delta_ref[...]))
        h_f = h[:, :HYENA_WIDTH] * decay
        row = lax.broadcasted_iota(jnp.int32, (l, HYENA_WIDTH), 0)
        h_b = jnp.where(row == 0, 0.0, h[:, HYENA_WIDTH:] * decay)
        norm = jnp.sum(jnp.abs(h_f) + jnp.abs(h_b), axis=0, keepdims=True)
        ks_ref[...] = (h_f + h_b) / norm
        kd_ref[...] = (h_f - h_b) / norm

    kre_ref[...] = jnp.dot(cp_ref[...], ks_ref[...], preferred_element_type=F32, precision=HIGHEST) * (1.0 / l)
    kim_ref[...] = jnp.dot(sp_ref[...], kd_ref[...], preferred_element_type=F32, precision=HIGHEST) * (-1.0 / l)


def _filter_spectrum(z, t, deltas, w1, b1, freq, w2, b2, w3, cp, sp):
    l = z.shape[0]
    fb = min(256, l)
    blk = pl.BlockSpec((fb, l), lambda i: (i, 0))
    out = pl.BlockSpec((fb, HYENA_WIDTH), lambda i: (i, 0))
    return pl.pallas_call(
        _filter_kernel,
        grid=(l // fb,),
        in_specs=[_whole(a.shape) for a in (z, t, deltas, w1, b1, freq, w2, b2, w3)] + [blk, blk],
        out_specs=[out, out],
        out_shape=[jax.ShapeDtypeStruct((l, HYENA_WIDTH), F32)] * 2,
        scratch_shapes=[pltpu.VMEM((l, HYENA_WIDTH), F32)] * 2,
        compiler_params=_params("arbitrary"),
        name="filter_spectrum",
    )(z, t, deltas, w1, b1, freq, w2, b2, w3, cp, sp)


def _long_conv_kernel(u_ref, m0_ref, cs_ref, ss_ref, kre_ref, kim_ref, bias_ref, o_ref):
    u = u_ref[0]
    ub = u.astype(BF16)
    a = jnp.dot(cs_ref[...], ub, preferred_element_type=F32)
    b = jnp.dot(ss_ref[...], ub, preferred_element_type=F32)
    kre, kim = kre_ref[...], kim_ref[...]
    zre = (a * kre + b * kim).astype(BF16)
    zim = (a * kim - b * kre).astype(BF16)
    y = (jnp.dot(cs_ref[...], zre, preferred_element_type=F32)
         - jnp.dot(ss_ref[...], zim, preferred_element_type=F32))
    o_ref[0] = ((y + u * bias_ref[...]) * m0_ref[0].astype(F32)).astype(BF16)


def _long_conv(u, m0, cs, ss, kre, kim, bias):
    b, l, _ = u.shape
    tok = pl.BlockSpec((1, l, HYENA_WIDTH), lambda i: (i, 0, 0))
    return pl.pallas_call(
        _long_conv_kernel,
        grid=(b,),
        in_specs=[tok, tok] + [_whole(a.shape) for a in (cs, ss, kre, kim, bias)],
        out_specs=tok,
        out_shape=jax.ShapeDtypeStruct((b, l, HYENA_WIDTH), BF16),
        compiler_params=_params("parallel"),
        name="long_conv",
    )(u, m0, cs, ss, kre, kim, bias)


def _outproj_kernel(yp_ref, yh_ref, ya_ref, x_ref, gate_ref, w_ref, g_ref, b_ref, o_ref, *, alpha):
    y = (jnp.dot(yp_ref[0], w_ref[0:POOL_WIDTH, :], preferred_element_type=F32)
         + jnp.dot(yh_ref[0], w_ref[POOL_WIDTH:POOL_WIDTH + HYENA_WIDTH, :], preferred_element_type=F32)
         + jnp.dot(ya_ref[0], w_ref[POOL_WIDTH + HYENA_WIDTH:, :], preferred_element_type=F32))
    o_ref[0] = _layernorm(alpha * x_ref[0] + gate_ref[0] * y) * g_ref[...] + b_ref[...]


def _out_projection(yp, yh, ya, x, mod, mod_row, w_out, ln_g, ln_b, alpha):
    b, l, d = x.shape
    tm = min(ROW_TILE, l)
    tok = lambda width: pl.BlockSpec((1, tm, width), lambda i, j: (i, j, 0))
    return pl.pallas_call(
        functools.partial(_outproj_kernel, alpha=alpha),
        grid=(b, l // tm),
        in_specs=[tok(POOL_WIDTH), tok(HYENA_WIDTH), tok(ATTN_WIDTH), tok(d),
                  pl.BlockSpec((1, 1, d), lambda i, j: (mod_row(i), 0, 2)),
                  _whole(w_out.shape), _whole(ln_g.shape), _whole(ln_b.shape)],
        out_specs=tok(d),
        out_shape=jax.ShapeDtypeStruct((b, l, d), F32),
        compiler_params=_params("parallel", "arbitrary"),
        name="out_projection",
    )(yp, yh, ya, x, mod, w_out, ln_g, ln_b)


def _rope_tables(l):
    rows = l // GRID_W
    row = jnp.repeat(jnp.arange(rows), GRID_W)
    col = jnp.tile(jnp.arange(GRID_W), rows)
    axis_dim = HEAD_DIM // 2
    inv = ROPE_THETA ** (-jnp.arange(0, axis_dim, 2, dtype=F32) / axis_dim)
    a0, a1 = row[:, None] * inv, col[:, None] * inv
    cos = jnp.concatenate([jnp.cos(a0)] * 2 + [jnp.cos(a1)] * 2, axis=-1)
    sin = jnp.concatenate([-jnp.sin(a0), jnp.sin(a0), -jnp.sin(a1), jnp.sin(a1)], axis=-1)
    reps = LANES // HEAD_DIM
    return jnp.tile(cos, (1, reps)), jnp.tile(sin, (1, reps))


def _trig_tables(l):
    odd = 2 * jnp.arange(l, dtype=jnp.int32) + 1
    hi = jnp.arange(l // LANES, dtype=jnp.int32) * LANES
    lo = jnp.arange(LANES, dtype=jnp.int32)

    def tables(col_hi, col_lo, period):
        def cos_sin(cols):
            ang = ((odd[:, None] * cols[None, :]) % period).astype(F32) * (2.0 * math.pi / period)
            return jnp.cos(ang), jnp.sin(ang)
        (ca, sa), (cb, sb) = cos_sin(col_hi), cos_sin(col_lo)
        ca, sa, cb, sb = ca[:, :, None], sa[:, :, None], cb[:, None, :], sb[:, None, :]
        return (ca * cb - sa * sb).reshape(l, l), (sa * cb + ca * sb).reshape(l, l)

    cs, ss = tables(2 * hi, 2 * lo + 1, 8 * l)
    cp, sp = tables(hi, lo, 4 * l)
    return cs.astype(BF16), ss.astype(BF16), cp, sp


def _filter_features(l):
    t = jnp.linspace(0.0, 1.0, l, dtype=F32)[:, None]
    bands = (HYENA_EMB - 1) // 2
    fr = jnp.linspace(1e-4, bands - 1, bands, dtype=F32)
    wpos = 2.0 * math.pi * jnp.arange(l, dtype=F32)[:, None] / l
    z = jnp.concatenate([t, jnp.cos(fr * wpos), -jnp.sin(fr * wpos)], axis=-1)
    z = jnp.pad(z, ((0, 0), (0, HYENA_EMB_PAD - HYENA_EMB)))
    max_decay = math.log(HYENA_TARGET) / HYENA_FAST_PCT
    min_decay = math.log(HYENA_TARGET) / HYENA_SLOW_PCT
    deltas = jnp.linspace(min_decay, max_decay, HYENA_WIDTH, dtype=F32)[None, :]
    return z, t, deltas


def kernel(x, c, ctx, c_ctx, w_ada, b_ada, w_in, pool_w, pool_scale, hy_conv_w, hy_conv_b, hy_w1, hy_b1,
           hy_freq, hy_w2, hy_b2, hy_w3, hy_bias, q_norm, k_norm, w_out, ln_g, ln_b):
    b, l, d = x.shape
    lc = ctx.shape[1]
    depth = w_in.shape[0]
    assert b + 1 <= MOD_ROWS and d == 1024 and w_in.shape[2] == IN_WIDTH
    alpha = (2.0 * depth) ** 0.25

    cc = jnp.concatenate([c, c_ctx[None, :], jnp.zeros((MOD_ROWS - b - 1, d), F32)], axis=0)
    mod_all = _ada_modulation(cc, w_ada, b_ada).reshape(depth, MOD_ROWS, 1, 3 * d)
    lat_row = lambda i: i
    ctx_row = lambda i: b

    head = jnp.arange(ATTN_WIDTH) // HEAD_DIM
    ones_bd = (head[:, None] == head[None, :]).astype(BF16)
    rope_tabs = _rope_tables(l)
    tables = {n: (_trig_tables(n), _filter_features(n)) for n in {l, lc}}

    for i in range(depth):
        last = i == depth - 1
        mod = mod_all[i]
        w_in_b = w_in[i].astype(BF16)
        w_out_b = w_out[i].astype(BF16)
        q_gain = jnp.tile(q_norm[i], ATTN_HEADS)[None, :]
        k_gain = jnp.tile(k_norm[i], KV_WIDTH // HEAD_DIM)[None, :]
        pool_bd = jax.scipy.linalg.block_diag(*pool_w[i]).astype(BF16)
        w1_pad = jnp.pad(hy_w1[i], ((0, HYENA_EMB_PAD - HYENA_EMB), (0, 0)))

        local_w = (hy_conv_w[i], hy_conv_b[i][None, :], pool_bd, pool_scale[i][None, :])

        def hyena(u, m0, n):
            (cs, ss, cp, sp), (z, t, deltas) = tables[n]
            kre, kim = _filter_spectrum(z, t, deltas, w1_pad, hy_b1[i][None, :], hy_freq[i], hy_w2[i],
                                        hy_b2[i][None, :], hy_w3[i], cp, sp)
            return _long_conv(u, m0, cs, ss, kre, kim, hy_bias[i][None, :])

        if last:
            ktc, vc = _kv_projection(ctx, mod, ctx_row, w_in_b, k_gain, ones_bd)
        else:
            ypc, uc, m0c, qc, ktc, vc, agc = _in_projection(
                ctx, mod, ctx_row, w_in_b, q_gain, k_gain, ones_bd, None, *local_w)

        yp, u, m0, q, kt, v, ag = _in_projection(
            x, mod, lat_row, w_in_b, q_gain, k_gain, ones_bd, rope_tabs, *local_w)
        attn_o = _attention(q, ag, [(kt, v), (ktc, vc)])
        x_new = _out_projection(yp, hyena(u, m0, l), attn_o, x, mod, lat_row, w_out_b,
                                ln_g[i][None, :], ln_b[i][None, :], alpha)
        if not last:
            attn_c = _attention(qc, agc, [(ktc, vc)])
            ctx = _out_projection(ypc, hyena(uc, m0c, lc), attn_c, ctx, mod, ctx_row, w_out_b,
                                  ln_g[i][None, :], ln_b[i][None, :], alpha)
        x = x_new
    return x
```

```python
import functools
import math

import jax
import jax.numpy as jnp
from jax import lax
from jax.experimental import pallas as pl
from jax.experimental.pallas import tpu as pltpu

F32 = jnp.float32
BF16 = jnp.bfloat16
HIGHEST = lax.Precision.HIGHEST

GRID_W = 64
POOL_WIDTH = 256
POOL_GROUPS = 4
POOL_HALF_WINDOWS = (1, 2, 4, 8)
HYENA_WIDTH = 256
HYENA_EMB = 33
HYENA_EMB_PAD = 40
HYENA_TARGET = 1e-2
HYENA_FAST_PCT = 0.3
HYENA_SLOW_PCT = 1.5
ATTN_WIDTH = 512
HEAD_DIM = 64
ATTN_HEADS = 8
ATTN_GROUP = 4
KV_WIDTH = 128
ROPE_THETA = 10000.0
LN_EPS = 1e-6
QK_EPS = 1e-6
LOG2_E = math.log2(math.e)

OFF_POOL_V, OFF_POOL_G, OFF_HY_U, OFF_HY_G, OFF_Q, OFF_K, OFF_V, OFF_ATTN_G, IN_WIDTH = (
    0, 256, 512, 1280, 1536, 2048, 2176, 2304, 2816)

LANES = 128
MOD_ROWS = 40
ROW_TILE = 512
OUT_ROW_TILE = 1024
HALO = 8
Q_TILE = 512
VMEM_LIMIT = 56 * 1024 * 1024


def _silu(x):
    return x * jax.nn.sigmoid(x)


def _layernorm(x):
    mu = jnp.mean(x, axis=-1, keepdims=True)
    xc = x - mu
    var = jnp.mean(xc * xc, axis=-1, keepdims=True)
    return xc * lax.rsqrt(var + LN_EPS)


def _params(*sem):
    return pltpu.CompilerParams(dimension_semantics=sem, vmem_limit_bytes=VMEM_LIMIT)


def _whole(shape):
    nd = len(shape)
    return pl.BlockSpec(shape, lambda *_: (0,) * nd, pipeline_mode=pl.Buffered(1))


def _ada_kernel(c_ref, w_ref, b_ref, o_ref):
    s = _silu(c_ref[...])
    o_ref[0] = jnp.dot(s, w_ref[0], preferred_element_type=F32, precision=HIGHEST) + b_ref[0]


def _ada_modulation(cc, w_ada, b_ada):
    depth, d, d3 = w_ada.shape
    tn = 512
    return pl.pallas_call(
        _ada_kernel,
        grid=(depth, d3 // tn),
        in_specs=[
            pl.BlockSpec((MOD_ROWS, d), lambda i, j: (0, 0)),
            pl.BlockSpec((1, d, tn), lambda i, j: (i, 0, j)),
            pl.BlockSpec((1, 1, tn), lambda i, j: (i, 0, j)),
        ],
        out_specs=pl.BlockSpec((1, MOD_ROWS, tn), lambda i, j: (i, 0, j)),
        out_shape=jax.ShapeDtypeStruct((depth, MOD_ROWS, d3), F32),
        compiler_params=_params("parallel", "parallel"),
        name="ada_modulation",
    )(cc, w_ada, b_ada.reshape(depth, 1, d3))


def _head_rms(t, ones_bd, gain):
    ss = jnp.dot((t * t).astype(BF16), ones_bd, preferred_element_type=F32)
    return t * lax.rsqrt(ss * (1.0 / HEAD_DIM) + QK_EPS) * gain


def _rope(t, cos, sin):
    lane = lax.broadcasted_iota(jnp.int32, t.shape, 1)
    partner = jnp.where((lane & 16) == 0, pltpu.roll(t, LANES - 16, 1), pltpu.roll(t, 16, 1))
    return t * cos + partner * sin


def _emit_kv(proj, rows, ones_ref, kg_ref, rope_refs, kt_ref, v_ref):
    k = _head_rms(proj(OFF_K, OFF_V), ones_ref[:KV_WIDTH, :KV_WIDTH], kg_ref[...])
    if rope_refs:
        k = _rope(k, rope_refs[0][...], rope_refs[1][...])
    kt_ref[0] = k.T.astype(BF16)
    v_ref[0, :, :KV_WIDTH] = proj(OFF_V, OFF_ATTN_G).astype(BF16)
    v_ref[0, :, KV_WIDTH:] = jnp.ones((rows, KV_WIDTH), BF16)


def _kv_proj_kernel(x_ref, shift_ref, scale_ref, w_ref, kg_ref, ones_ref, kt_ref, v_ref):
    hb = (_layernorm(x_ref[0]) * (1.0 + scale_ref[0]) + shift_ref[0]).astype(BF16)
    proj = lambda lo, hi: jnp.dot(hb, w_ref[:, lo:hi], preferred_element_type=F32)
    _emit_kv(proj, x_ref.shape[1], ones_ref, kg_ref, None, kt_ref, v_ref)


def _inproj_kernel(*refs, rope, seq_len):
    (x_ref, xp_ref, xn_ref, shift_ref, scale_ref, w_ref, qg_ref, kg_ref, ones_ref,
     cw_ref, cb_ref, pbd_ref, psc_ref) = refs[:13]
    rope_refs = refs[13:15] if rope else None
    yp_ref, u_ref, m0_ref, q_ref, kt_ref, v_ref, ag_ref = refs[15:] if rope else refs[13:]
    tm = x_ref.shape[1]
    te = tm + 2 * HALO

    xe = jnp.concatenate([xp_ref[0], x_ref[0], xn_ref[0]], axis=0)
    he = (_layernorm(xe) * (1.0 + scale_ref[0]) + shift_ref[0]).astype(BF16)
    hb = he[HALO:HALO + tm]
    proj = lambda lo, hi: jnp.dot(hb, w_ref[:, lo:hi], preferred_element_type=F32)
    proj_ext = lambda lo, hi: jnp.dot(he, w_ref[:, lo:hi], preferred_element_type=F32)

    pos = (pl.program_id(1) * tm - HALO) + lax.broadcasted_iota(jnp.int32, (te, LANES), 0)
    inside = (pos >= 0) & (pos < seq_len)
    first = lax.broadcasted_iota(jnp.int32, (te, LANES), 1) < LANES // 2
    up = lambda a, k: pltpu.roll(a, te - k, 0)
    down = lambda a, k: pltpu.roll(a, k, 0)
    core = lambda a: a[HALO:HALO + tm]

    pv = proj_ext(OFF_POOL_V, OFF_POOL_G)
    hu = proj_ext(OFF_HY_U, OFF_HY_G)
    q_raw = proj(OFF_Q, OFF_K)
    pool_gate_raw = proj(OFF_POOL_G, OFF_HY_U)
    hy_gate_raw = proj(OFF_HY_G, OFF_Q)
    ag_ref[0] = _silu(proj(OFF_ATTN_G, IN_WIDTH)).astype(BF16)
    q = _head_rms(q_raw, ones_ref[...], qg_ref[...] * (HEAD_DIM ** -0.5 * LOG2_E))
    for c in range(ATTN_WIDTH // LANES):
        qc = q[:, c * LANES:(c + 1) * LANES]
        if rope:
            qc = _rope(qc, rope_refs[0][...], rope_refs[1][...])
        q_ref[0, :, c * LANES:(c + 1) * LANES] = qc.astype(BF16)
    _emit_kv(proj, tm, ones_ref, kg_ref, rope_refs, kt_ref, v_ref)

    diffs = []
    for c in range(POOL_WIDTH // LANES):
        v = jnp.where(inside, pv[:, c * LANES:(c + 1) * LANES], 0.0)
        f2 = v + up(v, 1)
        f4 = f2 + up(f2, 2)
        if c == 0:
            wsum = jnp.where(first, down(f2, 1), down(f4, 2))
            half = jnp.where(first, POOL_HALF_WINDOWS[0], POOL_HALF_WINDOWS[1])
        else:
            f8 = f4 + up(f4, 4)
            wsum = jnp.where(first, down(f8, 4), down(f8, 8) + f8)
            half = jnp.where(first, POOL_HALF_WINDOWS[2], POOL_HALF_WINDOWS[3])
        cnt = jnp.maximum(jnp.minimum(pos + half, seq_len) - jnp.maximum(pos - half, 0), 1)
        diffs.append(core(wsum / cnt.astype(F32) - v).astype(BF16))
    y = jnp.dot(jnp.concatenate(diffs, axis=1), pbd_ref[...], preferred_element_type=F32)
    yp_ref[0] = (y * psc_ref[...] * _silu(pool_gate_raw)).astype(BF16)

    hgate = _silu(hy_gate_raw)
    ncol = HYENA_WIDTH // LANES
    conv = []
    for c in range(3 * ncol):
        cols = slice(c * LANES, (c + 1) * LANES)
        a = jnp.where(inside, hu[:, cols], 0.0)
        conv.append(core(cw_ref[0:1, cols] * down(a, 1) + cw_ref[1:2, cols] * a
                         + cw_ref[2:3, cols] * up(a, 1)) + cb_ref[:, cols])
    for c in range(ncol):
        cols = slice(c * LANES, (c + 1) * LANES)
        u_ref[0, :, cols] = conv[ncol + c] * conv[2 * ncol + c]
        m0_ref[0, :, cols] = (conv[c] * hgate[:, cols]).astype(BF16)


def _kv_specs(b, l, tm):
    tok = pl.BlockSpec((1, tm, 2 * KV_WIDTH), lambda i, j: (i, j, 0))
    kt_spec = pl.BlockSpec((1, KV_WIDTH, tm), lambda i, j: (i, 0, j))
    return ([kt_spec, tok],
            [jax.ShapeDtypeStruct((b, KV_WIDTH, l), BF16), jax.ShapeDtypeStruct((b, l, 2 * KV_WIDTH), BF16)])


def _mod_spec(d, mod_row, col):
    return pl.BlockSpec((1, 1, d), lambda i, j: (mod_row(i), 0, col))


def _kv_projection(x, mod, mod_row, w_in, k_gain, ones_bd):
    b, l, d = x.shape
    tm = min(ROW_TILE, l)
    out_specs, out_shape = _kv_specs(b, l, tm)
    return pl.pallas_call(
        _kv_proj_kernel,
        grid=(b, l // tm),
        in_specs=[pl.BlockSpec((1, tm, d), lambda i, j: (i, j, 0)), _mod_spec(d, mod_row, 0),
                  _mod_spec(d, mod_row, 1), _whole(w_in.shape), _whole(k_gain.shape), _whole(ones_bd.shape)],
        out_specs=out_specs,
        out_shape=out_shape,
        compiler_params=_params("parallel", "arbitrary"),
        name="kv_projection",
    )(x, mod, mod, w_in, k_gain, ones_bd)


def _in_projection(x, mod, mod_row, w_in, q_gain, k_gain, ones_bd, rope_tabs, conv_w, conv_b, pool_bd, pool_scale):
    b, l, d = x.shape
    tm = min(ROW_TILE, l)
    rope = rope_tabs is not None
    tok = lambda width: pl.BlockSpec((1, tm, width), lambda i, j: (i, j, 0))
    halo_blocks, last_halo = tm // HALO, l // HALO - 1
    consts = (w_in, q_gain, k_gain, ones_bd, conv_w, conv_b, pool_bd, pool_scale)
    in_specs = [
        tok(d),
        pl.BlockSpec((1, HALO, d), lambda i, j: (i, jnp.maximum(j * halo_blocks - 1, 0), 0)),
        pl.BlockSpec((1, HALO, d), lambda i, j: (i, jnp.minimum((j + 1) * halo_blocks, last_halo), 0)),
        _mod_spec(d, mod_row, 0), _mod_spec(d, mod_row, 1),
    ] + [_whole(a.shape) for a in consts]
    args = [x, x, x, mod, mod, *consts]
    if rope:
        in_specs += [pl.BlockSpec((tm, LANES), lambda i, j: (j, 0))] * 2
        args += list(rope_tabs)
    kv_specs, kv_shapes = _kv_specs(b, l, tm)
    widths = (POOL_WIDTH, HYENA_WIDTH, HYENA_WIDTH, ATTN_WIDTH)
    dtypes = (BF16, F32, BF16, BF16)
    return pl.pallas_call(
        functools.partial(_inproj_kernel, rope=rope, seq_len=l),
        grid=(b, l // tm),
        in_specs=in_specs,
        out_specs=[tok(w) for w in widths] + kv_specs + [tok(ATTN_WIDTH)],
        out_shape=([jax.ShapeDtypeStruct((b, l, w), t) for w, t in zip(widths, dtypes)]
                   + kv_shapes + [jax.ShapeDtypeStruct((b, l, ATTN_WIDTH), BF16)]),
        compiler_params=_params("parallel", "arbitrary"),
        name="in_projection",
    )(*args)


def _attn_kernel(*refs, n_src):
    q_ref, ag_ref = refs[:2]
    kt_refs = refs[2:2 + 2 * n_src:2]
    v_refs = refs[3:3 + 2 * n_src:2]
    o_ref = refs[2 + 2 * n_src]
    tq = q_ref.shape[1]
    lane = lax.broadcasted_iota(jnp.int32, (tq, LANES), 1)

    def head_scores(hq):
        pair, sub, kvh = hq // 2, hq % 2, hq // ATTN_GROUP
        qh = q_ref[0, :, pair * LANES:(pair + 1) * LANES][:, sub * HEAD_DIM:(sub + 1) * HEAD_DIM]
        return [jnp.dot(qh, kt[0, kvh * HEAD_DIM:(kvh + 1) * HEAD_DIM, :], preferred_element_type=F32)
                for kt in kt_refs]

    ahead = head_scores(0)
    halves = []
    for hq in range(ATTN_HEADS):
        pair, sub, kvh = hq // 2, hq % 2, hq // ATTN_GROUP
        scores = ahead
        if hq + 1 < ATTN_HEADS:
            ahead = head_scores(hq + 1)
        m = functools.reduce(jnp.maximum, [jnp.max(s, axis=-1, keepdims=True) for s in scores])
        probs = [jnp.exp2(s - m).astype(BF16) for s in scores]
        o = sum(jnp.dot(p, v[0], preferred_element_type=F32) for p, v in zip(probs, v_refs))
        o = o[:, :LANES] / o[:, LANES:]
        if kvh != sub:
            o = pltpu.roll(o, HEAD_DIM, 1)
        halves.append(o)
        if sub == 1:
            col = jnp.where(lane < HEAD_DIM, halves[-2], halves[-1])
            gate = ag_ref[0, :, pair * LANES:(pair + 1) * LANES].astype(F32)
            o_ref[0, :, pair * LANES:(pair + 1) * LANES] = (col * gate).astype(BF16)


def _attention(q, ag, kv_sources):
    b, l, _ = q.shape
    tq = min(Q_TILE, l)
    tok = pl.BlockSpec((1, tq, ATTN_WIDTH), lambda i, j: (i, j, 0))
    in_specs = [tok, tok]
    args = [q, ag]
    for kt, v in kv_sources:
        lk = v.shape[1]
        in_specs += [pl.BlockSpec((1, KV_WIDTH, lk), lambda i, j: (i, 0, 0)),
                     pl.BlockSpec((1, lk, 2 * KV_WIDTH), lambda i, j: (i, 0, 0))]
        args += [kt, v]
    return pl.pallas_call(
        functools.partial(_attn_kernel, n_src=len(kv_sources)),
        grid=(b, l // tq),
        in_specs=in_specs,
        out_specs=tok,
        out_shape=jax.ShapeDtypeStruct((b, l, ATTN_WIDTH), BF16),
        compiler_params=_params("parallel", "arbitrary"),
        name="attention",
    )(*args)


def _filter_kernel(z_ref, t_ref, delta_ref, w1_ref, b1_ref, fq_ref, w2_ref, b2_ref, w3_ref,
                   cp_ref, sp_ref, kre_ref, kim_ref, ks_ref, kd_ref):
    l = z_ref.shape[0]

    @pl.when(pl.program_id(0) == 0)
    def _():
        dot = functools.partial(jnp.dot, preferred_element_type=F32, precision=HIGHEST)
        h = jnp.sin(fq_ref[0:1, :] * (dot(z_ref[...], w1_ref[...]) + b1_ref[...]))
        h = jnp.sin(fq_ref[1:2, :] * (dot(h, w2_ref[...]) + b2_ref[...]))
        h = dot(h, w3_ref[...])
        decay = jnp.exp(-t_ref[...] * jnp.abs(delta_ref[...]))
        h_f = h[:, :HYENA_WIDTH] * decay
        row = lax.broadcasted_iota(jnp.int32, (l, HYENA_WIDTH), 0)
        h_b = jnp.where(row == 0, 0.0, h[:, HYENA_WIDTH:] * decay)
        norm = jnp.sum(jnp.abs(h_f) + jnp.abs(h_b), axis=0, keepdims=True)
        ks_ref[...] = (h_f + h_b) / norm
        kd_ref[...] = (h_f - h_b) / norm

    kre_ref[...] = jnp.dot(cp_ref[...], ks_ref[...], preferred_element_type=F32, precision=HIGHEST) * (1.0 / l)
    kim_ref[...] = jnp.dot(sp_ref[...], kd_ref[...], preferred_element_type=F32, precision=HIGHEST) * (-1.0 / l)


def _filter_spectrum(z, t, deltas, w1, b1, freq, w2, b2, w3, cp, sp):
    l = z.shape[0]
    fb = min(256, l)
    blk = pl.BlockSpec((fb, l), lambda i: (i, 0))
    out = pl.BlockSpec((fb, HYENA_WIDTH), lambda i: (i, 0))
    return pl.pallas_call(
        _filter_kernel,
        grid=(l // fb,),
        in_specs=[_whole(a.shape) for a in (z, t, deltas, w1, b1, freq, w2, b2, w3)] + [blk, blk],
        out_specs=[out, out],
        out_shape=[jax.ShapeDtypeStruct((l, HYENA_WIDTH), F32)] * 2,
        scratch_shapes=[pltpu.VMEM((l, HYENA_WIDTH), F32)] * 2,
        compiler_params=_params("arbitrary"),
        name="filter_spectrum",
    )(z, t, deltas, w1, b1, freq, w2, b2, w3, cp, sp)


def _long_conv_kernel(u_ref, m0_ref, cs_ref, ss_ref, kre_ref, kim_ref, bias_ref, o_ref):
    u = u_ref[0]
    ub = u.astype(BF16)
    a = jnp.dot(cs_ref[...], ub, preferred_element_type=F32)
    b = jnp.dot(ss_ref[...], ub, preferred_element_type=F32)
    kre, kim = kre_ref[...], kim_ref[...]
    zre = (a * kre + b * kim).astype(BF16)
    zim = (a * kim - b * kre).astype(BF16)
    y = (jnp.dot(cs_ref[...], zre, preferred_element_type=F32)
         - jnp.dot(ss_ref[...], zim, preferred_element_type=F32))
    o_ref[0] = ((y + u * bias_ref[...]) * m0_ref[0].astype(F32)).astype(BF16)


def _long_conv(u, m0, cs, ss, kre, kim, bias):
    b, l, _ = u.shape
    tok = pl.BlockSpec((1, l, HYENA_WIDTH), lambda i: (i, 0, 0))
    return pl.pallas_call(
        _long_conv_kernel,
        grid=(b,),
        in_specs=[tok, tok] + [_whole(a.shape) for a in (cs, ss, kre, kim, bias)],
        out_specs=tok,
        out_shape=jax.ShapeDtypeStruct((b, l, HYENA_WIDTH), BF16),
        compiler_params=_params("parallel"),
        name="long_conv",
    )(u, m0, cs, ss, kre, kim, bias)


def _outproj_kernel(yp_ref, yh_ref, ya_ref, x_ref, gate_ref, w_ref, g_ref, b_ref, o_ref, *, alpha):
    y = (jnp.dot(yp_ref[0], w_ref[0:POOL_WIDTH, :], preferred_element_type=F32)
         + jnp.dot(yh_ref[0], w_ref[POOL_WIDTH:POOL_WIDTH + HYENA_WIDTH, :], preferred_element_type=F32)
         + jnp.dot(ya_ref[0], w_ref[POOL_WIDTH + HYENA_WIDTH:, :], preferred_element_type=F32))
    o_ref[0] = _layernorm(alpha * x_ref[0] + gate_ref[0] * y) * g_ref[...] + b_ref[...]


def _out_projection(yp, yh, ya, x, mod, mod_row, w_out, ln_g, ln_b, alpha):
    b, l, d = x.shape
    tm = min(OUT_ROW_TILE, l)
    tok = lambda width: pl.BlockSpec((1, tm, width), lambda i, j: (i, j, 0))
    return pl.pallas_call(
        functools.partial(_outproj_kernel, alpha=alpha),
        grid=(b, l // tm),
        in_specs=[tok(POOL_WIDTH), tok(HYENA_WIDTH), tok(ATTN_WIDTH), tok(d),
                  pl.BlockSpec((1, 1, d), lambda i, j: (mod_row(i), 0, 2)),
                  _whole(w_out.shape), _whole(ln_g.shape), _whole(ln_b.shape)],
        out_specs=tok(d),
        out_shape=jax.ShapeDtypeStruct((b, l, d), F32),
        compiler_params=_params("parallel", "arbitrary"),
        name="out_projection",
    )(yp, yh, ya, x, mod, w_out, ln_g, ln_b)


def _rope_tables(l):
    rows = l // GRID_W
    row = jnp.repeat(jnp.arange(rows), GRID_W)
    col = jnp.tile(jnp.arange(GRID_W), rows)
    axis_dim = HEAD_DIM // 2
    inv = ROPE_THETA ** (-jnp.arange(0, axis_dim, 2, dtype=F32) / axis_dim)
    a0, a1 = row[:, None] * inv, col[:, None] * inv
    cos = jnp.concatenate([jnp.cos(a0)] * 2 + [jnp.cos(a1)] * 2, axis=-1)
    sin = jnp.concatenate([-jnp.sin(a0), jnp.sin(a0), -jnp.sin(a1), jnp.sin(a1)], axis=-1)
    reps = LANES // HEAD_DIM
    return jnp.tile(cos, (1, reps)), jnp.tile(sin, (1, reps))


def _trig_tables(l):
    odd = 2 * jnp.arange(l, dtype=jnp.int32) + 1
    hi = jnp.arange(l // LANES, dtype=jnp.int32) * LANES
    lo = jnp.arange(LANES, dtype=jnp.int32)

    def tables(col_hi, col_lo, period):
        def cos_sin(cols):
            ang = ((odd[:, None] * cols[None, :]) % period).astype(F32) * (2.0 * math.pi / period)
            return jnp.cos(ang), jnp.sin(ang)
        (ca, sa), (cb, sb) = cos_sin(col_hi), cos_sin(col_lo)
        ca, sa, cb, sb = ca[:, :, None], sa[:, :, None], cb[:, None, :], sb[:, None, :]
        return (ca * cb - sa * sb).reshape(l, l), (sa * cb + ca * sb).reshape(l, l)

    cs, ss = tables(2 * hi, 2 * lo + 1, 8 * l)
    cp, sp = tables(hi, lo, 4 * l)
    return cs.astype(BF16), ss.astype(BF16), cp, sp


def _filter_features(l):
    t = jnp.linspace(0.0, 1.0, l, dtype=F32)[:, None]
    bands = (HYENA_EMB - 1) // 2
    fr = jnp.linspace(1e-4, bands - 1, bands, dtype=F32)
    wpos = 2.0 * math.pi * jnp.arange(l, dtype=F32)[:, None] / l
    z = jnp.concatenate([t, jnp.cos(fr * wpos), -jnp.sin(fr * wpos)], axis=-1)
    z = jnp.pad(z, ((0, 0), (0, HYENA_EMB_PAD - HYENA_EMB)))
    max_decay = math.log(HYENA_TARGET) / HYENA_FAST_PCT
    min_decay = math.log(HYENA_TARGET) / HYENA_SLOW_PCT
    deltas = jnp.linspace(min_decay, max_decay, HYENA_WIDTH, dtype=F32)[None, :]
    return z, t, deltas


def kernel(x, c, ctx, c_ctx, w_ada, b_ada, w_in, pool_w, pool_scale, hy_conv_w, hy_conv_b, hy_w1, hy_b1,
           hy_freq, hy_w2, hy_b2, hy_w3, hy_bias, q_norm, k_norm, w_out, ln_g, ln_b):
    b, l, d = x.shape
    lc = ctx.shape[1]
    depth = w_in.shape[0]
    assert b + 1 <= MOD_ROWS and d == 1024 and w_in.shape[2] == IN_WIDTH
    alpha = (2.0 * depth) ** 0.25

    cc = jnp.concatenate([c, c_ctx[None, :], jnp.zeros((MOD_ROWS - b - 1, d), F32)], axis=0)
    mod_all = _ada_modulation(cc, w_ada, b_ada).reshape(depth, MOD_ROWS, 1, 3 * d)
    lat_row = lambda i: i
    ctx_row = lambda i: b

    head = jnp.arange(ATTN_WIDTH) // HEAD_DIM
    ones_bd = (head[:, None] == head[None, :]).astype(BF16)
    rope_tabs = _rope_tables(l)
    tables = {n: (_trig_tables(n), _filter_features(n)) for n in {l, lc}}

    for i in range(depth):
        last = i == depth - 1
        mod = mod_all[i]
        w_in_b = w_in[i].astype(BF16)
        w_out_b = w_out[i].astype(BF16)
        q_gain = jnp.tile(q_norm[i], ATTN_HEADS)[None, :]
        k_gain = jnp.tile(k_norm[i], KV_WIDTH // HEAD_DIM)[None, :]
        pool_bd = jax.scipy.linalg.block_diag(*pool_w[i]).astype(BF16)
        w1_pad = jnp.pad(hy_w1[i], ((0, HYENA_EMB_PAD - HYENA_EMB), (0, 0)))

        local_w = (hy_conv_w[i], hy_conv_b[i][None, :], pool_bd, pool_scale[i][None, :])

        def hyena(u, m0, n):
            (cs, ss, cp, sp), (z, t, deltas) = tables[n]
            kre, kim = _filter_spectrum(z, t, deltas, w1_pad, hy_b1[i][None, :], hy_freq[i], hy_w2[i],
                                        hy_b2[i][None, :], hy_w3[i], cp, sp)
            return _long_conv(u, m0, cs, ss, kre, kim, hy_bias[i][None, :])

        if last:
            ktc, vc = _kv_projection(ctx, mod, ctx_row, w_in_b, k_gain, ones_bd)
        else:
            ypc, uc, m0c, qc, ktc, vc, agc = _in_projection(
                ctx, mod, ctx_row, w_in_b, q_gain, k_gain, ones_bd, None, *local_w)

        yp, u, m0, q, kt, v, ag = _in_projection(
            x, mod, lat_row, w_in_b, q_gain, k_gain, ones_bd, rope_tabs, *local_w)
        attn_o = _attention(q, ag, [(kt, v), (ktc, vc)])
        x_new = _out_projection(yp, hyena(u, m0, l), attn_o, x, mod, lat_row, w_out_b,
                                ln_g[i][None, :], ln_b[i][None, :], alpha)
        if not last:
            attn_c = _attention(qc, agc, [(ktc, vc)])
            ctx = _out_projection(ypc, hyena(uc, m0c, lc), attn_c, ctx, mod, ctx_row, w_out_b,
                                  ln_g[i][None, :], ln_b[i][None, :], alpha)
        x = x_new
    return x
```

```python
import functools
import math

import jax
import jax.numpy as jnp
from jax import lax
from jax.experimental import pallas as pl
from jax.experimental.pallas import tpu as pltpu

F32 = jnp.float32
BF16 = jnp.bfloat16
HIGHEST = lax.Precision.HIGHEST

GRID_W = 64
POOL_WIDTH = 256
POOL_GROUPS = 4
POOL_HALF_WINDOWS = (1, 2, 4, 8)
HYENA_WIDTH = 256
HYENA_EMB = 33
HYENA_EMB_PAD = 40
HYENA_TARGET = 1e-2
HYENA_FAST_PCT = 0.3
HYENA_SLOW_PCT = 1.5
ATTN_WIDTH = 512
HEAD_DIM = 64
ATTN_HEADS = 8
ATTN_GROUP = 4
KV_WIDTH = 128
ROPE_THETA = 10000.0
LN_EPS = 1e-6
QK_EPS = 1e-6
LOG2_E = math.log2(math.e)

OFF_POOL_V, OFF_POOL_G, OFF_HY_U, OFF_HY_G, OFF_Q, OFF_K, OFF_V, OFF_ATTN_G, IN_WIDTH = (
    0, 256, 512, 1280, 1536, 2048, 2176, 2304, 2816)

LANES = 128
MOD_ROWS = 40
ROW_TILE = 1024
OUT_ROW_TILE = 1024
HALO = 8
Q_TILE = 512
VMEM_LIMIT = 56 * 1024 * 1024


def _silu(x):
    return x * jax.nn.sigmoid(x)


def _layernorm(x):
    mu = jnp.mean(x, axis=-1, keepdims=True)
    xc = x - mu
    var = jnp.mean(xc * xc, axis=-1, keepdims=True)
    return xc * lax.rsqrt(var + LN_EPS)


def _params(*sem):
    return pltpu.CompilerParams(dimension_semantics=sem, vmem_limit_bytes=VMEM_LIMIT)


def _whole(shape):
    nd = len(shape)
    return pl.BlockSpec(shape, lambda *_: (0,) * nd, pipeline_mode=pl.Buffered(1))


def _ada_kernel(c_ref, w_ref, b_ref, o_ref):
    s = _silu(c_ref[...])
    o_ref[0] = jnp.dot(s, w_ref[0], preferred_element_type=F32, precision=HIGHEST) + b_ref[0]


def _ada_modulation(cc, w_ada, b_ada):
    depth, d, d3 = w_ada.shape
    tn = 512
    return pl.pallas_call(
        _ada_kernel,
        grid=(depth, d3 // tn),
        in_specs=[
            pl.BlockSpec((MOD_ROWS, d), lambda i, j: (0, 0)),
            pl.BlockSpec((1, d, tn), lambda i, j: (i, 0, j)),
            pl.BlockSpec((1, 1, tn), lambda i, j: (i, 0, j)),
        ],
        out_specs=pl.BlockSpec((1, MOD_ROWS, tn), lambda i, j: (i, 0, j)),
        out_shape=jax.ShapeDtypeStruct((depth, MOD_ROWS, d3), F32),
        compiler_params=_params("parallel", "parallel"),
        name="ada_modulation",
    )(cc, w_ada, b_ada.reshape(depth, 1, d3))


def _head_rms(t, ones_bd, gain):
    ss = jnp.dot((t * t).astype(BF16), ones_bd, preferred_element_type=F32)
    return t * lax.rsqrt(ss * (1.0 / HEAD_DIM) + QK_EPS) * gain


def _rope(t, cos, sin):
    lane = lax.broadcasted_iota(jnp.int32, t.shape, 1)
    partner = jnp.where((lane & 16) == 0, pltpu.roll(t, LANES - 16, 1), pltpu.roll(t, 16, 1))
    return t * cos + partner * sin


def _emit_kv(proj, rows, ones_ref, kg_ref, rope_refs, kt_ref, v_ref):
    k = _head_rms(proj(OFF_K, OFF_V), ones_ref[:KV_WIDTH, :KV_WIDTH], kg_ref[...])
    if rope_refs:
        k = _rope(k, rope_refs[0][...], rope_refs[1][...])
    kt_ref[0] = k.T.astype(BF16)
    v_ref[0, :, :KV_WIDTH] = proj(OFF_V, OFF_ATTN_G).astype(BF16)
    v_ref[0, :, KV_WIDTH:] = jnp.ones((rows, KV_WIDTH), BF16)


def _kv_proj_kernel(x_ref, shift_ref, scale_ref, w_ref, kg_ref, ones_ref, kt_ref, v_ref):
    hb = (_layernorm(x_ref[0]) * (1.0 + scale_ref[0]) + shift_ref[0]).astype(BF16)
    proj = lambda lo, hi: jnp.dot(hb, w_ref[:, lo:hi], preferred_element_type=F32)
    _emit_kv(proj, x_ref.shape[1], ones_ref, kg_ref, None, kt_ref, v_ref)


def _inproj_kernel(*refs, rope, seq_len):
    (x_ref, xp_ref, xn_ref, shift_ref, scale_ref, w_ref, qg_ref, kg_ref, ones_ref,
     cw_ref, cb_ref, pbd_ref, psc_ref) = refs[:13]
    rope_refs = refs[13:15] if rope else None
    yp_ref, u_ref, m0_ref, q_ref, kt_ref, v_ref, ag_ref = refs[15:] if rope else refs[13:]
    tm = x_ref.shape[1]
    te = tm + 2 * HALO

    xe = jnp.concatenate([xp_ref[0], x_ref[0], xn_ref[0]], axis=0)
    he = (_layernorm(xe) * (1.0 + scale_ref[0]) + shift_ref[0]).astype(BF16)
    hb = he[HALO:HALO + tm]
    proj = lambda lo, hi: jnp.dot(hb, w_ref[:, lo:hi], preferred_element_type=F32)
    proj_ext = lambda lo, hi: jnp.dot(he, w_ref[:, lo:hi], preferred_element_type=F32)

    pos = (pl.program_id(1) * tm - HALO) + lax.broadcasted_iota(jnp.int32, (te, LANES), 0)
    inside = (pos >= 0) & (pos < seq_len)
    first = lax.broadcasted_iota(jnp.int32, (te, LANES), 1) < LANES // 2
    up = lambda a, k: pltpu.roll(a, te - k, 0)
    down = lambda a, k: pltpu.roll(a, k, 0)
    core = lambda a: a[HALO:HALO + tm]

    pv = proj_ext(OFF_POOL_V, OFF_POOL_G)
    hu = proj_ext(OFF_HY_U, OFF_HY_G)
    q_raw = proj(OFF_Q, OFF_K)
    pool_gate_raw = proj(OFF_POOL_G, OFF_HY_U)
    hy_gate_raw = proj(OFF_HY_G, OFF_Q)
    ag_ref[0] = _silu(proj(OFF_ATTN_G, IN_WIDTH)).astype(BF16)
    q = _head_rms(q_raw, ones_ref[...], qg_ref[...] * (HEAD_DIM ** -0.5 * LOG2_E))
    for c in range(ATTN_WIDTH // LANES):
        qc = q[:, c * LANES:(c + 1) * LANES]
        if rope:
            qc = _rope(qc, rope_refs[0][...], rope_refs[1][...])
        q_ref[0, :, c * LANES:(c + 1) * LANES] = qc.astype(BF16)
    _emit_kv(proj, tm, ones_ref, kg_ref, rope_refs, kt_ref, v_ref)

    diffs = []
    for c in range(POOL_WIDTH // LANES):
        v = jnp.where(inside, pv[:, c * LANES:(c + 1) * LANES], 0.0)
        f2 = v + up(v, 1)
        f4 = f2 + up(f2, 2)
        if c == 0:
            wsum = jnp.where(first, down(f2, 1), down(f4, 2))
            half = jnp.where(first, POOL_HALF_WINDOWS[0], POOL_HALF_WINDOWS[1])
        else:
            f8 = f4 + up(f4, 4)
            wsum = jnp.where(first, down(f8, 4), down(f8, 8) + f8)
            half = jnp.where(first, POOL_HALF_WINDOWS[2], POOL_HALF_WINDOWS[3])
        cnt = jnp.maximum(jnp.minimum(pos + half, seq_len) - jnp.maximum(pos - half, 0), 1)
        diffs.append(core(wsum / cnt.astype(F32) - v).astype(BF16))
    y = jnp.dot(jnp.concatenate(diffs, axis=1), pbd_ref[...], preferred_element_type=F32)
    yp_ref[0] = (y * psc_ref[...] * _silu(pool_gate_raw)).astype(BF16)

    hgate = _silu(hy_gate_raw)
    ncol = HYENA_WIDTH // LANES
    conv = []
    for c in range(3 * ncol):
        cols = slice(c * LANES, (c + 1) * LANES)
        a = jnp.where(inside, hu[:, cols], 0.0)
        conv.append(core(cw_ref[0:1, cols] * down(a, 1) + cw_ref[1:2, cols] * a
                         + cw_ref[2:3, cols] * up(a, 1)) + cb_ref[:, cols])
    for c in range(ncol):
        cols = slice(c * LANES, (c + 1) * LANES)
        u_ref[0, :, cols] = conv[ncol + c] * conv[2 * ncol + c]
        m0_ref[0, :, cols] = (conv[c] * hgate[:, cols]).astype(BF16)


def _kv_specs(b, l, tm):
    tok = pl.BlockSpec((1, tm, 2 * KV_WIDTH), lambda i, j: (i, j, 0))
    kt_spec = pl.BlockSpec((1, KV_WIDTH, tm), lambda i, j: (i, 0, j))
    return ([kt_spec, tok],
            [jax.ShapeDtypeStruct((b, KV_WIDTH, l), BF16), jax.ShapeDtypeStruct((b, l, 2 * KV_WIDTH), BF16)])


def _mod_spec(d, mod_row, col):
    return pl.BlockSpec((1, 1, d), lambda i, j: (mod_row(i), 0, col))


def _kv_projection(x, mod, mod_row, w_in, k_gain, ones_bd):
    b, l, d = x.shape
    tm = min(ROW_TILE, l)
    out_specs, out_shape = _kv_specs(b, l, tm)
    return pl.pallas_call(
        _kv_proj_kernel,
        grid=(b, l // tm),
        in_specs=[pl.BlockSpec((1, tm, d), lambda i, j: (i, j, 0)), _mod_spec(d, mod_row, 0),
                  _mod_spec(d, mod_row, 1), _whole(w_in.shape), _whole(k_gain.shape), _whole(ones_bd.shape)],
        out_specs=out_specs,
        out_shape=out_shape,
        compiler_params=_params("parallel", "arbitrary"),
        name="kv_projection",
    )(x, mod, mod, w_in, k_gain, ones_bd)


def _in_projection(x, mod, mod_row, w_in, q_gain, k_gain, ones_bd, rope_tabs, conv_w, conv_b, pool_bd, pool_scale):
    b, l, d = x.shape
    tm = min(ROW_TILE, l)
    rope = rope_tabs is not None
    tok = lambda width: pl.BlockSpec((1, tm, width), lambda i, j: (i, j, 0))
    halo_blocks, last_halo = tm // HALO, l // HALO - 1
    consts = (w_in, q_gain, k_gain, ones_bd, conv_w, conv_b, pool_bd, pool_scale)
    in_specs = [
        tok(d),
        pl.BlockSpec((1, HALO, d), lambda i, j: (i, jnp.maximum(j * halo_blocks - 1, 0), 0)),
        pl.BlockSpec((1, HALO, d), lambda i, j: (i, jnp.minimum((j + 1) * halo_blocks, last_halo), 0)),
        _mod_spec(d, mod_row, 0), _mod_spec(d, mod_row, 1),
    ] + [_whole(a.shape) for a in consts]
    args = [x, x, x, mod, mod, *consts]
    if rope:
        in_specs += [pl.BlockSpec((tm, LANES), lambda i, j: (j, 0))] * 2
        args += list(rope_tabs)
    kv_specs, kv_shapes = _kv_specs(b, l, tm)
    widths = (POOL_WIDTH, HYENA_WIDTH, HYENA_WIDTH, ATTN_WIDTH)
    dtypes = (BF16, F32, BF16, BF16)
    return pl.pallas_call(
        functools.partial(_inproj_kernel, rope=rope, seq_len=l),
        grid=(b, l // tm),
        in_specs=in_specs,
        out_specs=[tok(w) for w in widths] + kv_specs + [tok(ATTN_WIDTH)],
        out_shape=([jax.ShapeDtypeStruct((b, l, w), t) for w, t in zip(widths, dtypes)]
                   + kv_shapes + [jax.ShapeDtypeStruct((b, l, ATTN_WIDTH), BF16)]),
        compiler_params=_params("parallel", "arbitrary"),
        name="in_projection",
    )(*args)


def _attn_kernel(*refs, n_src):
    q_ref, ag_ref = refs[:2]
    kt_refs = refs[2:2 + 2 * n_src:2]
    v_refs = refs[3:3 + 2 * n_src:2]
    o_ref = refs[2 + 2 * n_src]
    tq = q_ref.shape[1]
    lane = lax.broadcasted_iota(jnp.int32, (tq, LANES), 1)

    def head_scores(hq):
        pair, sub, kvh = hq // 2, hq % 2, hq // ATTN_GROUP
        qh = q_ref[0, :, pair * LANES:(pair + 1) * LANES][:, sub * HEAD_DIM:(sub + 1) * HEAD_DIM]
        return [jnp.dot(qh, kt[0, kvh * HEAD_DIM:(kvh + 1) * HEAD_DIM, :], preferred_element_type=F32)
                for kt in kt_refs]

    ahead = head_scores(0)
    halves = []
    for hq in range(ATTN_HEADS):
        pair, sub, kvh = hq // 2, hq % 2, hq // ATTN_GROUP
        scores = ahead
        if hq + 1 < ATTN_HEADS:
            ahead = head_scores(hq + 1)
        m = functools.reduce(jnp.maximum, [jnp.max(s, axis=-1, keepdims=True) for s in scores])
        probs = [jnp.exp2(s - m).astype(BF16) for s in scores]
        o = sum(jnp.dot(p, v[0], preferred_element_type=F32) for p, v in zip(probs, v_refs))
        o = o[:, :LANES] / o[:, LANES:]
        if kvh != sub:
            o = pltpu.roll(o, HEAD_DIM, 1)
        halves.append(o)
        if sub == 1:
            col = jnp.where(lane < HEAD_DIM, halves[-2], halves[-1])
            gate = ag_ref[0, :, pair * LANES:(pair + 1) * LANES].astype(F32)
            o_ref[0, :, pair * LANES:(pair + 1) * LANES] = (col * gate).astype(BF16)


def _attention(q, ag, kv_sources):
    b, l, _ = q.shape
    tq = min(Q_TILE, l)
    tok = pl.BlockSpec((1, tq, ATTN_WIDTH), lambda i, j: (i, j, 0))
    in_specs = [tok, tok]
    args = [q, ag]
    for kt, v in kv_sources:
        lk = v.shape[1]
        in_specs += [pl.BlockSpec((1, KV_WIDTH, lk), lambda i, j: (i, 0, 0)),
                     pl.BlockSpec((1, lk, 2 * KV_WIDTH), lambda i, j: (i, 0, 0))]
        args += [kt, v]
    return pl.pallas_call(
        functools.partial(_attn_kernel, n_src=len(kv_sources)),
        grid=(b, l // tq),
        in_specs=in_specs,
        out_specs=tok,
        out_shape=jax.ShapeDtypeStruct((b, l, ATTN_WIDTH), BF16),
        compiler_params=_params("parallel", "arbitrary"),
        name="attention",
    )(*args)


def _filter_kernel(z_ref, t_ref, delta_ref, w1_ref, b1_ref, fq_ref, w2_ref, b2_ref, w3_ref,
                   cp_ref, sp_ref, kre_ref, kim_ref, ks_ref, kd_ref):
    l = z_ref.shape[0]

    @pl.when(pl.program_id(0) == 0)
    def _():
        dot = functools.partial(jnp.dot, preferred_element_type=F32, precision=HIGHEST)
        h = jnp.sin(fq_ref[0:1, :] * (dot(z_ref[...], w1_ref[...]) + b1_ref[...]))
        h = jnp.sin(fq_ref[1:2, :] * (dot(h, w2_ref[...]) + b2_ref[...]))
        h = dot(h, w3_ref[...])
        decay = jnp.exp(-t_ref[...] * jnp.abs(delta_ref[...]))
        h_f = h[:, :HYENA_WIDTH] * decay
        row = lax.broadcasted_iota(jnp.int32, (l, HYENA_WIDTH), 0)
        h_b = jnp.where(row == 0, 0.0, h[:, HYENA_WIDTH:] * decay)
        norm = jnp.sum(jnp.abs(h_f) + jnp.abs(h_b), axis=0, keepdims=True)
        ks_ref[...] = (h_f + h_b) / norm
        kd_ref[...] = (h_f - h_b) / norm

    def split(a):
        hi = a.astype(BF16)
        return hi, (a - hi.astype(F32)).astype(BF16)

    def dot3(a, b):
        (a_hi, a_lo), (b_hi, b_lo) = split(a), split(b)
        d = functools.partial(jnp.dot, preferred_element_type=F32)
        return d(a_hi, b_hi) + (d(a_hi, b_lo) + d(a_lo, b_hi))

    kre_ref[...] = dot3(cp_ref[...], ks_ref[...]) * (1.0 / l)
    kim_ref[...] = dot3(sp_ref[...], kd_ref[...]) * (-1.0 / l)


def _filter_spectrum(z, t, deltas, w1, b1, freq, w2, b2, w3, cp, sp):
    l = z.shape[0]
    fb = min(256, l)
    blk = pl.BlockSpec((fb, l), lambda i: (i, 0))
    out = pl.BlockSpec((fb, HYENA_WIDTH), lambda i: (i, 0))
    return pl.pallas_call(
        _filter_kernel,
        grid=(l // fb,),
        in_specs=[_whole(a.shape) for a in (z, t, deltas, w1, b1, freq, w2, b2, w3)] + [blk, blk],
        out_specs=[out, out],
        out_shape=[jax.ShapeDtypeStruct((l, HYENA_WIDTH), F32)] * 2,
        scratch_shapes=[pltpu.VMEM((l, HYENA_WIDTH), F32)] * 2,
        compiler_params=_params("arbitrary"),
        name="filter_spectrum",
    )(z, t, deltas, w1, b1, freq, w2, b2, w3, cp, sp)


def _long_conv_kernel(u_ref, m0_ref, pe_ref, me_ref, po_ref, mo_ref, pet_ref, met_ref, pot_ref, mot_ref, rev_ref,
                      kre_e_ref, kim_e_ref, kre_o_ref, kim_o_ref, bias_ref, o_ref):
    h = u_ref.shape[1] // 2
    dot = functools.partial(jnp.dot, preferred_element_type=F32)
    u = u_ref[0]
    mirrored = dot(rev_ref[...], u[h:].astype(BF16))
    up = (u[:h] + mirrored).astype(BF16)
    um = (u[:h] - mirrored).astype(BF16)

    def spectrum(a_plus_b, a_minus_b, kre, kim):
        a2, b2 = a_plus_b + a_minus_b, a_plus_b - a_minus_b
        zre, zim = a2 * kre + b2 * kim, a2 * kim - b2 * kre
        return (zre - zim).astype(BF16), (zre + zim).astype(BF16)

    dm_e, dp_e = spectrum(dot(pe_ref[...], up), dot(me_ref[...], um), kre_e_ref[...], kim_e_ref[...])
    dm_o, dp_o = spectrum(dot(po_ref[...], um), dot(mo_ref[...], up), kre_o_ref[...], kim_o_ref[...])
    y_sum = dot(pet_ref[...], dm_e) + dot(mot_ref[...], dp_o)
    y_dif = dot(met_ref[...], dp_e) + dot(pot_ref[...], dm_o)
    y_lo = (y_sum + y_dif) * 0.25
    y_hi = dot(rev_ref[...], ((y_sum - y_dif) * 0.25).astype(BF16))
    bias = bias_ref[...]
    o_ref[0, :h] = ((y_lo + u[:h] * bias) * m0_ref[0, :h].astype(F32)).astype(BF16)
    o_ref[0, h:] = ((y_hi + u[h:] * bias) * m0_ref[0, h:].astype(F32)).astype(BF16)


def _long_conv(u, m0, blocks, kre, kim, bias):
    b, l, _ = u.shape
    tok = pl.BlockSpec((1, l, HYENA_WIDTH), lambda i: (i, 0, 0))
    consts = (*blocks, kre[0::2], kim[0::2], kre[1::2], kim[1::2], bias)
    return pl.pallas_call(
        _long_conv_kernel,
        grid=(b,),
        in_specs=[tok, tok] + [_whole(a.shape) for a in consts],
        out_specs=tok,
        out_shape=jax.ShapeDtypeStruct((b, l, HYENA_WIDTH), BF16),
        compiler_params=_params("parallel"),
        name="long_conv",
    )(u, m0, *consts)


def _outproj_kernel(yp_ref, yh_ref, ya_ref, x_ref, gate_ref, w_ref, g_ref, b_ref, o_ref, *, alpha):
    y = (jnp.dot(yp_ref[0], w_ref[0:POOL_WIDTH, :], preferred_element_type=F32)
         + jnp.dot(yh_ref[0], w_ref[POOL_WIDTH:POOL_WIDTH + HYENA_WIDTH, :], preferred_element_type=F32)
         + jnp.dot(ya_ref[0], w_ref[POOL_WIDTH + HYENA_WIDTH:, :], preferred_element_type=F32))
    o_ref[0] = _layernorm(alpha * x_ref[0] + gate_ref[0] * y) * g_ref[...] + b_ref[...]


def _out_projection(yp, yh, ya, x, mod, mod_row, w_out, ln_g, ln_b, alpha):
    b, l, d = x.shape
    tm = min(OUT_ROW_TILE, l)
    tok = lambda width: pl.BlockSpec((1, tm, width), lambda i, j: (i, j, 0))
    return pl.pallas_call(
        functools.partial(_outproj_kernel, alpha=alpha),
        grid=(b, l // tm),
        in_specs=[tok(POOL_WIDTH), tok(HYENA_WIDTH), tok(ATTN_WIDTH), tok(d),
                  pl.BlockSpec((1, 1, d), lambda i, j: (mod_row(i), 0, 2)),
                  _whole(w_out.shape), _whole(ln_g.shape), _whole(ln_b.shape)],
        out_specs=tok(d),
        out_shape=jax.ShapeDtypeStruct((b, l, d), F32),
        compiler_params=_params("parallel", "arbitrary"),
        name="out_projection",
    )(yp, yh, ya, x, mod, w_out, ln_g, ln_b)


def _rope_tables(l):
    rows = l // GRID_W
    row = jnp.repeat(jnp.arange(rows), GRID_W)
    col = jnp.tile(jnp.arange(GRID_W), rows)
    axis_dim = HEAD_DIM // 2
    inv = ROPE_THETA ** (-jnp.arange(0, axis_dim, 2, dtype=F32) / axis_dim)
    a0, a1 = row[:, None] * inv, col[:, None] * inv
    cos = jnp.concatenate([jnp.cos(a0)] * 2 + [jnp.cos(a1)] * 2, axis=-1)
    sin = jnp.concatenate([-jnp.sin(a0), jnp.sin(a0), -jnp.sin(a1), jnp.sin(a1)], axis=-1)
    reps = LANES // HEAD_DIM
    return jnp.tile(cos, (1, reps)), jnp.tile(sin, (1, reps))


def _trig_tables(l):
    odd = 2 * jnp.arange(l, dtype=jnp.int32) + 1
    hi = jnp.arange(l // LANES, dtype=jnp.int32) * LANES
    lo = jnp.arange(LANES, dtype=jnp.int32)

    def tables(col_hi, col_lo, period):
        def cos_sin(cols):
            ang = ((odd[:, None] * cols[None, :]) % period).astype(F32) * (2.0 * math.pi / period)
            return jnp.cos(ang), jnp.sin(ang)
        (ca, sa), (cb, sb) = cos_sin(col_hi), cos_sin(col_lo)
        ca, sa, cb, sb = ca[:, :, None], sa[:, :, None], cb[:, None, :], sb[:, None, :]
        return (ca * cb - sa * sb).reshape(l, l), (sa * cb + ca * sb).reshape(l, l)

    cs, ss = tables(2 * hi, 2 * lo + 1, 8 * l)
    cp, sp = tables(hi, lo, 4 * l)
    h = l // 2
    p, m = cs + ss, cs - ss
    fwd = [p[0::2, :h], m[0::2, :h], p[1::2, :h], m[1::2, :h]]
    blocks = [a.astype(BF16) for a in fwd + [a.T for a in fwd]]
    rev = (jnp.arange(h)[:, None] + jnp.arange(h)[None, :] == h - 1).astype(BF16)
    return (*blocks, rev), cp, sp


def _filter_features(l):
    t = jnp.linspace(0.0, 1.0, l, dtype=F32)[:, None]
    bands = (HYENA_EMB - 1) // 2
    fr = jnp.linspace(1e-4, bands - 1, bands, dtype=F32)
    wpos = 2.0 * math.pi * jnp.arange(l, dtype=F32)[:, None] / l
    z = jnp.concatenate([t, jnp.cos(fr * wpos), -jnp.sin(fr * wpos)], axis=-1)
    z = jnp.pad(z, ((0, 0), (0, HYENA_EMB_PAD - HYENA_EMB)))
    max_decay = math.log(HYENA_TARGET) / HYENA_FAST_PCT
    min_decay = math.log(HYENA_TARGET) / HYENA_SLOW_PCT
    deltas = jnp.linspace(min_decay, max_decay, HYENA_WIDTH, dtype=F32)[None, :]
    return z, t, deltas


def kernel(x, c, ctx, c_ctx, w_ada, b_ada, w_in, pool_w, pool_scale, hy_conv_w, hy_conv_b, hy_w1, hy_b1,
           hy_freq, hy_w2, hy_b2, hy_w3, hy_bias, q_norm, k_norm, w_out, ln_g, ln_b):
    b, l, d = x.shape
    lc = ctx.shape[1]
    depth = w_in.shape[0]
    assert b + 1 <= MOD_ROWS and d == 1024 and w_in.shape[2] == IN_WIDTH
    alpha = (2.0 * depth) ** 0.25

    cc = jnp.concatenate([c, c_ctx[None, :], jnp.zeros((MOD_ROWS - b - 1, d), F32)], axis=0)
    mod_all = _ada_modulation(cc, w_ada, b_ada).reshape(depth, MOD_ROWS, 1, 3 * d)
    lat_row = lambda i: i
    ctx_row = lambda i: b

    head = jnp.arange(ATTN_WIDTH) // HEAD_DIM
    ones_bd = (head[:, None] == head[None, :]).astype(BF16)
    rope_tabs = _rope_tables(l)
    tables = {n: (_trig_tables(n), _filter_features(n)) for n in {l, lc}}

    for i in range(depth):
        last = i == depth - 1
        mod = mod_all[i]
        w_in_b = w_in[i].astype(BF16)
        w_out_b = w_out[i].astype(BF16)
        q_gain = jnp.tile(q_norm[i], ATTN_HEADS)[None, :]
        k_gain = jnp.tile(k_norm[i], KV_WIDTH // HEAD_DIM)[None, :]
        pool_bd = jax.scipy.linalg.block_diag(*pool_w[i]).astype(BF16)
        w1_pad = jnp.pad(hy_w1[i], ((0, HYENA_EMB_PAD - HYENA_EMB), (0, 0)))

        local_w = (hy_conv_w[i], hy_conv_b[i][None, :], pool_bd, pool_scale[i][None, :])

        def hyena(u, m0, n):
            (blocks, cp, sp), (z, t, deltas) = tables[n]
            kre, kim = _filter_spectrum(z, t, deltas, w1_pad, hy_b1[i][None, :], hy_freq[i], hy_w2[i],
                                        hy_b2[i][None, :], hy_w3[i], cp, sp)
            return _long_conv(u, m0, blocks, kre, kim, hy_bias[i][None, :])

        if last:
            ktc, vc = _kv_projection(ctx, mod, ctx_row, w_in_b, k_gain, ones_bd)
        else:
            ypc, uc, m0c, qc, ktc, vc, agc = _in_projection(
                ctx, mod, ctx_row, w_in_b, q_gain, k_gain, ones_bd, None, *local_w)

        yp, u, m0, q, kt, v, ag = _in_projection(
            x, mod, lat_row, w_in_b, q_gain, k_gain, ones_bd, rope_tabs, *local_w)
        attn_o = _attention(q, ag, [(kt, v), (ktc, vc)])
        x_new = _out_projection(yp, hyena(u, m0, l), attn_o, x, mod, lat_row, w_out_b,
                                ln_g[i][None, :], ln_b[i][None, :], alpha)
        if not last:
            attn_c = _attention(qc, agc, [(ktc, vc)])
            ctx = _out_projection(ypc, hyena(uc, m0c, lc), attn_c, ctx, mod, ctx_row, w_out_b,
                                  ln_g[i][None, :], ln_b[i][None, :], alpha)
        x = x_new
    return x
```

```python
import functools
import math

import jax
import jax.numpy as jnp
from jax import lax
from jax.experimental import pallas as pl
from jax.experimental.pallas import tpu as pltpu

F32 = jnp.float32
BF16 = jnp.bfloat16
HIGHEST = lax.Precision.HIGHEST

GRID_W = 64
POOL_WIDTH = 256
POOL_GROUPS = 4
POOL_HALF_WINDOWS = (1, 2, 4, 8)
HYENA_WIDTH = 256
HYENA_EMB = 33
HYENA_EMB_PAD = 40
HYENA_TARGET = 1e-2
HYENA_FAST_PCT = 0.3
HYENA_SLOW_PCT = 1.5
ATTN_WIDTH = 512
HEAD_DIM = 64
ATTN_HEADS = 8
ATTN_GROUP = 4
KV_WIDTH = 128
ROPE_THETA = 10000.0
LN_EPS = 1e-6
QK_EPS = 1e-6
LOG2_E = math.log2(math.e)

OFF_POOL_V, OFF_POOL_G, OFF_HY_U, OFF_HY_G, OFF_Q, OFF_K, OFF_V, OFF_ATTN_G, IN_WIDTH = (
    0, 256, 512, 1280, 1536, 2048, 2176, 2304, 2816)

LANES = 128
MOD_ROWS = 40
ROW_TILE = 1024
OUT_ROW_TILE = 1024
HALO = 8
Q_TILE = 512
VMEM_LIMIT = 56 * 1024 * 1024


def _silu(x):
    return x * jax.nn.sigmoid(x)


def _layernorm(x):
    mu = jnp.mean(x, axis=-1, keepdims=True)
    xc = x - mu
    var = jnp.mean(xc * xc, axis=-1, keepdims=True)
    return xc * lax.rsqrt(var + LN_EPS)


def _params(*sem):
    return pltpu.CompilerParams(dimension_semantics=sem, vmem_limit_bytes=VMEM_LIMIT)


def _whole(shape):
    nd = len(shape)
    return pl.BlockSpec(shape, lambda *_: (0,) * nd, pipeline_mode=pl.Buffered(1))


def _ada_kernel(c_ref, w_ref, b_ref, o_ref):
    s = _silu(c_ref[...])
    o_ref[0] = jnp.dot(s, w_ref[0], preferred_element_type=F32, precision=HIGHEST) + b_ref[0]


def _ada_modulation(cc, w_ada, b_ada):
    depth, d, d3 = w_ada.shape
    tn = 512
    return pl.pallas_call(
        _ada_kernel,
        grid=(depth, d3 // tn),
        in_specs=[
            pl.BlockSpec((MOD_ROWS, d), lambda i, j: (0, 0)),
            pl.BlockSpec((1, d, tn), lambda i, j: (i, 0, j)),
            pl.BlockSpec((1, 1, tn), lambda i, j: (i, 0, j)),
        ],
        out_specs=pl.BlockSpec((1, MOD_ROWS, tn), lambda i, j: (i, 0, j)),
        out_shape=jax.ShapeDtypeStruct((depth, MOD_ROWS, d3), F32),
        compiler_params=_params("parallel", "parallel"),
        name="ada_modulation",
    )(cc, w_ada, b_ada.reshape(depth, 1, d3))


def _head_rms(t, ones_bd, gain):
    ss = jnp.dot((t * t).astype(BF16), ones_bd, preferred_element_type=F32)
    return t * lax.rsqrt(ss * (1.0 / HEAD_DIM) + QK_EPS) * gain


def _rope(t, cos, sin):
    lane = lax.broadcasted_iota(jnp.int32, t.shape, 1)
    partner = jnp.where((lane & 16) == 0, pltpu.roll(t, LANES - 16, 1), pltpu.roll(t, 16, 1))
    return t * cos + partner * sin


def _emit_kv(proj, rows, ones_ref, kg_ref, rope_refs, kt_ref, v_ref):
    k = _head_rms(proj(OFF_K, OFF_V), ones_ref[:KV_WIDTH, :KV_WIDTH], kg_ref[...])
    if rope_refs:
        k = _rope(k, rope_refs[0][...], rope_refs[1][...])
    kt_ref[0] = k.T.astype(BF16)
    v_ref[0, :, :KV_WIDTH] = proj(OFF_V, OFF_ATTN_G).astype(BF16)
    v_ref[0, :, KV_WIDTH:] = jnp.ones((rows, KV_WIDTH), BF16)


def _kv_proj_kernel(x_ref, shift_ref, scale_ref, w_ref, kg_ref, ones_ref, kt_ref, v_ref):
    hb = (_layernorm(x_ref[0]) * (1.0 + scale_ref[0]) + shift_ref[0]).astype(BF16)
    proj = lambda lo, hi: jnp.dot(hb, w_ref[:, lo:hi], preferred_element_type=F32)
    _emit_kv(proj, x_ref.shape[1], ones_ref, kg_ref, None, kt_ref, v_ref)


def _inproj_kernel(*refs, rope, seq_len):
    (x_ref, xp_ref, xn_ref, shift_ref, scale_ref, w_ref, qg_ref, kg_ref, ones_ref,
     cw_ref, cb_ref, pbd_ref, psc_ref) = refs[:13]
    rope_refs = refs[13:15] if rope else None
    yp_ref, u_ref, m0_ref, q_ref, kt_ref, v_ref, ag_ref = refs[15:] if rope else refs[13:]
    tm = x_ref.shape[1]
    te = tm + 2 * HALO

    xe = jnp.concatenate([xp_ref[0], x_ref[0], xn_ref[0]], axis=0)
    he = (_layernorm(xe) * (1.0 + scale_ref[0]) + shift_ref[0]).astype(BF16)
    hb = he[HALO:HALO + tm]
    proj = lambda lo, hi: jnp.dot(hb, w_ref[:, lo:hi], preferred_element_type=F32)
    proj_ext = lambda lo, hi: jnp.dot(he, w_ref[:, lo:hi], preferred_element_type=F32)

    pos = (pl.program_id(1) * tm - HALO) + lax.broadcasted_iota(jnp.int32, (te, LANES), 0)
    inside = (pos >= 0) & (pos < seq_len)
    first = lax.broadcasted_iota(jnp.int32, (te, LANES), 1) < LANES // 2
    up = lambda a, k: pltpu.roll(a, te - k, 0)
    down = lambda a, k: pltpu.roll(a, k, 0)
    core = lambda a: a[HALO:HALO + tm]

    pv = proj_ext(OFF_POOL_V, OFF_POOL_G)
    hu = proj_ext(OFF_HY_U, OFF_HY_G)
    q_raw = proj(OFF_Q, OFF_K)
    pool_gate_raw = proj(OFF_POOL_G, OFF_HY_U)
    hy_gate_raw = proj(OFF_HY_G, OFF_Q)
    ag_ref[0] = _silu(proj(OFF_ATTN_G, IN_WIDTH)).astype(BF16)
    q = _head_rms(q_raw, ones_ref[...], qg_ref[...] * (HEAD_DIM ** -0.5 * LOG2_E))
    for c in range(ATTN_WIDTH // LANES):
        qc = q[:, c * LANES:(c + 1) * LANES]
        if rope:
            qc = _rope(qc, rope_refs[0][...], rope_refs[1][...])
        q_ref[0, :, c * LANES:(c + 1) * LANES] = qc.astype(BF16)
    _emit_kv(proj, tm, ones_ref, kg_ref, rope_refs, kt_ref, v_ref)

    diffs = []
    for c in range(POOL_WIDTH // LANES):
        v = jnp.where(inside, pv[:, c * LANES:(c + 1) * LANES], 0.0)
        f2 = v + up(v, 1)
        f4 = f2 + up(f2, 2)
        if c == 0:
            wsum = jnp.where(first, down(f2, 1), down(f4, 2))
            half = jnp.where(first, POOL_HALF_WINDOWS[0], POOL_HALF_WINDOWS[1])
        else:
            f8 = f4 + up(f4, 4)
            wsum = jnp.where(first, down(f8, 4), down(f8, 8) + f8)
            half = jnp.where(first, POOL_HALF_WINDOWS[2], POOL_HALF_WINDOWS[3])
        cnt = jnp.maximum(jnp.minimum(pos + half, seq_len) - jnp.maximum(pos - half, 0), 1)
        diffs.append(core(wsum / cnt.astype(F32) - v).astype(BF16))
    y = jnp.dot(jnp.concatenate(diffs, axis=1), pbd_ref[...], preferred_element_type=F32)
    yp_ref[0] = (y * psc_ref[...] * _silu(pool_gate_raw)).astype(BF16)

    hgate = _silu(hy_gate_raw)
    ncol = HYENA_WIDTH // LANES
    conv = []
    for c in range(3 * ncol):
        cols = slice(c * LANES, (c + 1) * LANES)
        a = jnp.where(inside, hu[:, cols], 0.0)
        conv.append(core(cw_ref[0:1, cols] * down(a, 1) + cw_ref[1:2, cols] * a
                         + cw_ref[2:3, cols] * up(a, 1)) + cb_ref[:, cols])
    for c in range(ncol):
        cols = slice(c * LANES, (c + 1) * LANES)
        u_ref[0, :, cols] = conv[ncol + c] * conv[2 * ncol + c]
        m0_ref[0, :, cols] = (conv[c] * hgate[:, cols]).astype(BF16)


def _kv_specs(b, l, tm):
    tok = pl.BlockSpec((1, tm, 2 * KV_WIDTH), lambda i, j: (i, j, 0))
    kt_spec = pl.BlockSpec((1, KV_WIDTH, tm), lambda i, j: (i, 0, j))
    return ([kt_spec, tok],
            [jax.ShapeDtypeStruct((b, KV_WIDTH, l), BF16), jax.ShapeDtypeStruct((b, l, 2 * KV_WIDTH), BF16)])


def _mod_spec(d, mod_row, col):
    return pl.BlockSpec((1, 1, d), lambda i, j: (mod_row(i), 0, col))


def _kv_projection(x, mod, mod_row, w_in, k_gain, ones_bd):
    b, l, d = x.shape
    tm = min(ROW_TILE, l)
    out_specs, out_shape = _kv_specs(b, l, tm)
    return pl.pallas_call(
        _kv_proj_kernel,
        grid=(b, l // tm),
        in_specs=[pl.BlockSpec((1, tm, d), lambda i, j: (i, j, 0)), _mod_spec(d, mod_row, 0),
                  _mod_spec(d, mod_row, 1), _whole(w_in.shape), _whole(k_gain.shape), _whole(ones_bd.shape)],
        out_specs=out_specs,
        out_shape=out_shape,
        compiler_params=_params("parallel", "arbitrary"),
        name="kv_projection",
    )(x, mod, mod, w_in, k_gain, ones_bd)


def _in_projection(x, mod, mod_row, w_in, q_gain, k_gain, ones_bd, rope_tabs, conv_w, conv_b, pool_bd, pool_scale):
    b, l, d = x.shape
    tm = min(ROW_TILE, l)
    rope = rope_tabs is not None
    tok = lambda width: pl.BlockSpec((1, tm, width), lambda i, j: (i, j, 0))
    halo_blocks, last_halo = tm // HALO, l // HALO - 1
    consts = (w_in, q_gain, k_gain, ones_bd, conv_w, conv_b, pool_bd, pool_scale)
    in_specs = [
        tok(d),
        pl.BlockSpec((1, HALO, d), lambda i, j: (i, jnp.maximum(j * halo_blocks - 1, 0), 0)),
        pl.BlockSpec((1, HALO, d), lambda i, j: (i, jnp.minimum((j + 1) * halo_blocks, last_halo), 0)),
        _mod_spec(d, mod_row, 0), _mod_spec(d, mod_row, 1),
    ] + [_whole(a.shape) for a in consts]
    args = [x, x, x, mod, mod, *consts]
    if rope:
        in_specs += [pl.BlockSpec((tm, LANES), lambda i, j: (j, 0))] * 2
        args += list(rope_tabs)
    kv_specs, kv_shapes = _kv_specs(b, l, tm)
    widths = (POOL_WIDTH, HYENA_WIDTH, HYENA_WIDTH, ATTN_WIDTH)
    dtypes = (BF16, F32, BF16, BF16)
    return pl.pallas_call(
        functools.partial(_inproj_kernel, rope=rope, seq_len=l),
        grid=(b, l // tm),
        in_specs=in_specs,
        out_specs=[tok(w) for w in widths] + kv_specs + [tok(ATTN_WIDTH)],
        out_shape=([jax.ShapeDtypeStruct((b, l, w), t) for w, t in zip(widths, dtypes)]
                   + kv_shapes + [jax.ShapeDtypeStruct((b, l, ATTN_WIDTH), BF16)]),
        compiler_params=_params("parallel", "arbitrary"),
        name="in_projection",
    )(*args)


def _attn_kernel(*refs, n_src):
    q_ref, ag_ref = refs[:2]
    kt_refs = refs[2:2 + 2 * n_src:2]
    v_refs = refs[3:3 + 2 * n_src:2]
    o_ref = refs[2 + 2 * n_src]
    tq = q_ref.shape[1]
    lane = lax.broadcasted_iota(jnp.int32, (tq, LANES), 1)

    def head_scores(hq):
        pair, sub, kvh = hq // 2, hq % 2, hq // ATTN_GROUP
        qh = q_ref[0, :, pair * LANES:(pair + 1) * LANES][:, sub * HEAD_DIM:(sub + 1) * HEAD_DIM]
        return [jnp.dot(qh, kt[0, kvh * HEAD_DIM:(kvh + 1) * HEAD_DIM, :], preferred_element_type=F32)
                for kt in kt_refs]

    ahead = head_scores(0)
    halves = []
    for hq in range(ATTN_HEADS):
        pair, sub, kvh = hq // 2, hq % 2, hq // ATTN_GROUP
        scores = ahead
        if hq + 1 < ATTN_HEADS:
            ahead = head_scores(hq + 1)
        m = functools.reduce(jnp.maximum, [jnp.max(s, axis=-1, keepdims=True) for s in scores])
        probs = [jnp.exp2(s - m).astype(BF16) for s in scores]
        o = sum(jnp.dot(p, v[0], preferred_element_type=F32) for p, v in zip(probs, v_refs))
        o = o[:, :LANES] / o[:, LANES:]
        if kvh != sub:
            o = pltpu.roll(o, HEAD_DIM, 1)
        halves.append(o)
        if sub == 1:
            col = jnp.where(lane < HEAD_DIM, halves[-2], halves[-1])
            gate = ag_ref[0, :, pair * LANES:(pair + 1) * LANES].astype(F32)
            o_ref[0, :, pair * LANES:(pair + 1) * LANES] = (col * gate).astype(BF16)


def _attention(q, ag, kv_sources):
    b, l, _ = q.shape
    tq = min(Q_TILE, l)
    tok = pl.BlockSpec((1, tq, ATTN_WIDTH), lambda i, j: (i, j, 0))
    in_specs = [tok, tok]
    args = [q, ag]
    for kt, v in kv_sources:
        lk = v.shape[1]
        in_specs += [pl.BlockSpec((1, KV_WIDTH, lk), lambda i, j: (i, 0, 0)),
                     pl.BlockSpec((1, lk, 2 * KV_WIDTH), lambda i, j: (i, 0, 0))]
        args += [kt, v]
    return pl.pallas_call(
        functools.partial(_attn_kernel, n_src=len(kv_sources)),
        grid=(b, l // tq),
        in_specs=in_specs,
        out_specs=tok,
        out_shape=jax.ShapeDtypeStruct((b, l, ATTN_WIDTH), BF16),
        compiler_params=_params("parallel", "arbitrary"),
        name="attention",
    )(*args)


def _filter_kernel(z_ref, t_ref, delta_ref, w1_ref, b1_ref, fq_ref, w2_ref, b2_ref, w3_ref,
                   cp_ref, sp_ref, kre_ref, kim_ref, ks_ref, kd_ref):
    l = z_ref.shape[0]

    @pl.when(pl.program_id(0) == 0)
    def _():
        dot = functools.partial(jnp.dot, preferred_element_type=F32, precision=HIGHEST)
        h = jnp.sin(fq_ref[0:1, :] * (dot(z_ref[...], w1_ref[...]) + b1_ref[...]))
        h = jnp.sin(fq_ref[1:2, :] * (dot(h, w2_ref[...]) + b2_ref[...]))
        h = dot(h, w3_ref[...])
        decay = jnp.exp(-t_ref[...] * jnp.abs(delta_ref[...]))
        h_f = h[:, :HYENA_WIDTH] * decay
        row = lax.broadcasted_iota(jnp.int32, (l, HYENA_WIDTH), 0)
        h_b = jnp.where(row == 0, 0.0, h[:, HYENA_WIDTH:] * decay)
        norm = jnp.sum(jnp.abs(h_f) + jnp.abs(h_b), axis=0, keepdims=True)
        ks_ref[...] = (h_f + h_b) / norm
        kd_ref[...] = (h_f - h_b) / norm

    def split(a):
        hi = a.astype(BF16)
        return hi, (a - hi.astype(F32)).astype(BF16)

    def dot3(a, b):
        (a_hi, a_lo), (b_hi, b_lo) = split(a), split(b)
        d = functools.partial(jnp.dot, preferred_element_type=F32)
        return d(a_hi, b_hi) + (d(a_hi, b_lo) + d(a_lo, b_hi))

    kre_ref[...] = dot3(cp_ref[...], ks_ref[...]) * (1.0 / l)
    kim_ref[...] = dot3(sp_ref[...], kd_ref[...]) * (-1.0 / l)


def _filter_spectrum(z, t, deltas, w1, b1, freq, w2, b2, w3, cp, sp):
    l = z.shape[0]
    fb = min(256, l)
    blk = pl.BlockSpec((fb, l), lambda i: (i, 0))
    out = pl.BlockSpec((fb, HYENA_WIDTH), lambda i: (i, 0))
    return pl.pallas_call(
        _filter_kernel,
        grid=(l // fb,),
        in_specs=[_whole(a.shape) for a in (z, t, deltas, w1, b1, freq, w2, b2, w3)] + [blk, blk],
        out_specs=[out, out],
        out_shape=[jax.ShapeDtypeStruct((l, HYENA_WIDTH), F32)] * 2,
        scratch_shapes=[pltpu.VMEM((l, HYENA_WIDTH), F32)] * 2,
        compiler_params=_params("arbitrary"),
        name="filter_spectrum",
    )(z, t, deltas, w1, b1, freq, w2, b2, w3, cp, sp)


def _long_conv_kernel(u_ref, m0_ref, pe_ref, me_ref, po_ref, mo_ref, pet_ref, met_ref, pot_ref, mot_ref, rev_ref,
                      kre_e_ref, kim_e_ref, kre_o_ref, kim_o_ref, bias_ref, o_ref):
    h = u_ref.shape[1] // 2
    dot = functools.partial(jnp.dot, preferred_element_type=F32)
    u = u_ref[0]
    mirrored = dot(rev_ref[...], u[h:].astype(BF16))
    up = (u[:h] + mirrored).astype(BF16)
    um = (u[:h] - mirrored).astype(BF16)

    def spectrum(a_plus_b, a_minus_b, kre, kim):
        a2, b2 = a_plus_b + a_minus_b, a_plus_b - a_minus_b
        zre, zim = a2 * kre + b2 * kim, a2 * kim - b2 * kre
        return (zre - zim).astype(BF16), (zre + zim).astype(BF16)

    dm_e, dp_e = spectrum(dot(pe_ref[...], up), dot(me_ref[...], um), kre_e_ref[...], kim_e_ref[...])
    dm_o, dp_o = spectrum(dot(po_ref[...], um), dot(mo_ref[...], up), kre_o_ref[...], kim_o_ref[...])
    y_sum = dot(pet_ref[...], dm_e) + dot(mot_ref[...], dp_o)
    y_dif = dot(met_ref[...], dp_e) + dot(pot_ref[...], dm_o)
    y_lo = (y_sum + y_dif) * 0.25
    y_hi = dot(rev_ref[...], ((y_sum - y_dif) * 0.25).astype(BF16))
    bias = bias_ref[...]
    o_ref[0, :h] = ((y_lo + u[:h] * bias) * m0_ref[0, :h].astype(F32)).astype(BF16)
    o_ref[0, h:] = ((y_hi + u[h:] * bias) * m0_ref[0, h:].astype(F32)).astype(BF16)


def _long_conv(u, m0, blocks, kre, kim, bias):
    b, l, _ = u.shape
    tok = pl.BlockSpec((1, l, HYENA_WIDTH), lambda i: (i, 0, 0))
    consts = (*blocks, kre[0::2], kim[0::2], kre[1::2], kim[1::2], bias)
    return pl.pallas_call(
        _long_conv_kernel,
        grid=(b,),
        in_specs=[tok, tok] + [_whole(a.shape) for a in consts],
        out_specs=tok,
        out_shape=jax.ShapeDtypeStruct((b, l, HYENA_WIDTH), BF16),
        compiler_params=_params("parallel"),
        name="long_conv",
    )(u, m0, *consts)


def _outproj_kernel(yp_ref, yh_ref, ya_ref, x_ref, gate_ref, w_ref, g_ref, b_ref, o_ref, *, alpha):
    y = (jnp.dot(yp_ref[0], w_ref[0:POOL_WIDTH, :], preferred_element_type=F32)
         + jnp.dot(yh_ref[0], w_ref[POOL_WIDTH:POOL_WIDTH + HYENA_WIDTH, :], preferred_element_type=F32)
         + jnp.dot(ya_ref[0], w_ref[POOL_WIDTH + HYENA_WIDTH:, :], preferred_element_type=F32))
    o_ref[0] = _layernorm(alpha * x_ref[0] + gate_ref[0] * y) * g_ref[...] + b_ref[...]


def _out_projection(yp, yh, ya, x, mod, mod_row, w_out, ln_g, ln_b, alpha):
    b, l, d = x.shape
    tm = min(OUT_ROW_TILE, l)
    tok = lambda width: pl.BlockSpec((1, tm, width), lambda i, j: (i, j, 0))
    return pl.pallas_call(
        functools.partial(_outproj_kernel, alpha=alpha),
        grid=(b, l // tm),
        in_specs=[tok(POOL_WIDTH), tok(HYENA_WIDTH), tok(ATTN_WIDTH), tok(d),
                  pl.BlockSpec((1, 1, d), lambda i, j: (mod_row(i), 0, 2)),
                  _whole(w_out.shape), _whole(ln_g.shape), _whole(ln_b.shape)],
        out_specs=tok(d),
        out_shape=jax.ShapeDtypeStruct((b, l, d), F32),
        compiler_params=_params("parallel", "arbitrary"),
        name="out_projection",
    )(yp, yh, ya, x, mod, w_out, ln_g, ln_b)


def _rope_tables(l):
    rows = l // GRID_W
    row = jnp.repeat(jnp.arange(rows), GRID_W)
    col = jnp.tile(jnp.arange(GRID_W), rows)
    axis_dim = HEAD_DIM // 2
    inv = ROPE_THETA ** (-jnp.arange(0, axis_dim, 2, dtype=F32) / axis_dim)
    a0, a1 = row[:, None] * inv, col[:, None] * inv
    cos = jnp.concatenate([jnp.cos(a0)] * 2 + [jnp.cos(a1)] * 2, axis=-1)
    sin = jnp.concatenate([-jnp.sin(a0), jnp.sin(a0), -jnp.sin(a1), jnp.sin(a1)], axis=-1)
    reps = LANES // HEAD_DIM
    return jnp.tile(cos, (1, reps)), jnp.tile(sin, (1, reps))


def _trig_tables(l):
    h = l // 2
    ar = lambda n: jnp.arange(n, dtype=jnp.int32)

    def tables(rows, col_hi, col_lo, period):
        def cos_sin(cols):
            ang = ((rows[:, None] * cols[None, :]) % period).astype(F32) * (2.0 * math.pi / period)
            return jnp.cos(ang), jnp.sin(ang)
        (ca, sa), (cb, sb) = cos_sin(col_hi), cos_sin(col_lo)
        ca, sa, cb, sb = ca[:, :, None], sa[:, :, None], cb[:, None, :], sb[:, None, :]
        shape = (rows.shape[0], col_hi.shape[0] * col_lo.shape[0])
        return (ca * cb - sa * sb).reshape(shape), (sa * cb + ca * sb).reshape(shape)

    def plus_minus(rows, col_hi, col_lo):
        cs, ss = tables(rows, col_hi, col_lo, 8 * l)
        return (cs + ss).astype(BF16), (cs - ss).astype(BF16)

    lo = ar(min(LANES, h))
    hi = ar(h // lo.shape[0]) * lo.shape[0]
    pe, me = plus_minus(4 * ar(h) + 1, 2 * hi, 2 * lo + 1)
    po, mo = plus_minus(4 * ar(h) + 3, 2 * hi, 2 * lo + 1)
    pet, met = plus_minus(2 * ar(h) + 1, 4 * hi, 4 * lo + 1)
    pot, mot = plus_minus(2 * ar(h) + 1, 4 * hi, 4 * lo + 3)
    rev = (ar(h)[:, None] + ar(h)[None, :] == h - 1).astype(BF16)
    lo_full = ar(LANES)
    cp, sp = tables(2 * ar(l) + 1, ar(l // LANES) * LANES, lo_full, 4 * l)
    return (pe, me, po, mo, pet, met, pot, mot, rev), cp, sp


def _filter_features(l):
    t = jnp.linspace(0.0, 1.0, l, dtype=F32)[:, None]
    bands = (HYENA_EMB - 1) // 2
    fr = jnp.linspace(1e-4, bands - 1, bands, dtype=F32)
    wpos = 2.0 * math.pi * jnp.arange(l, dtype=F32)[:, None] / l
    z = jnp.concatenate([t, jnp.cos(fr * wpos), -jnp.sin(fr * wpos)], axis=-1)
    z = jnp.pad(z, ((0, 0), (0, HYENA_EMB_PAD - HYENA_EMB)))
    max_decay = math.log(HYENA_TARGET) / HYENA_FAST_PCT
    min_decay = math.log(HYENA_TARGET) / HYENA_SLOW_PCT
    deltas = jnp.linspace(min_decay, max_decay, HYENA_WIDTH, dtype=F32)[None, :]
    return z, t, deltas


def kernel(x, c, ctx, c_ctx, w_ada, b_ada, w_in, pool_w, pool_scale, hy_conv_w, hy_conv_b, hy_w1, hy_b1,
           hy_freq, hy_w2, hy_b2, hy_w3, hy_bias, q_norm, k_norm, w_out, ln_g, ln_b):
    b, l, d = x.shape
    lc = ctx.shape[1]
    depth = w_in.shape[0]
    assert b + 1 <= MOD_ROWS and d == 1024 and w_in.shape[2] == IN_WIDTH
    alpha = (2.0 * depth) ** 0.25

    cc = jnp.concatenate([c, c_ctx[None, :], jnp.zeros((MOD_ROWS - b - 1, d), F32)], axis=0)
    mod_all = _ada_modulation(cc, w_ada, b_ada).reshape(depth, MOD_ROWS, 1, 3 * d)
    lat_row = lambda i: i
    ctx_row = lambda i: b

    head = jnp.arange(ATTN_WIDTH) // HEAD_DIM
    ones_bd = (head[:, None] == head[None, :]).astype(BF16)
    rope_tabs = _rope_tables(l)
    tables = {n: (_trig_tables(n), _filter_features(n)) for n in {l, lc}}

    for i in range(depth):
        last = i == depth - 1
        mod = mod_all[i]
        w_in_b = w_in[i].astype(BF16)
        w_out_b = w_out[i].astype(BF16)
        q_gain = jnp.tile(q_norm[i], ATTN_HEADS)[None, :]
        k_gain = jnp.tile(k_norm[i], KV_WIDTH // HEAD_DIM)[None, :]
        pool_bd = jax.scipy.linalg.block_diag(*pool_w[i]).astype(BF16)
        w1_pad = jnp.pad(hy_w1[i], ((0, HYENA_EMB_PAD - HYENA_EMB), (0, 0)))

        local_w = (hy_conv_w[i], hy_conv_b[i][None, :], pool_bd, pool_scale[i][None, :])

        def hyena(u, m0, n):
            (blocks, cp, sp), (z, t, deltas) = tables[n]
            kre, kim = _filter_spectrum(z, t, deltas, w1_pad, hy_b1[i][None, :], hy_freq[i], hy_w2[i],
                                        hy_b2[i][None, :], hy_w3[i], cp, sp)
            return _long_conv(u, m0, blocks, kre, kim, hy_bias[i][None, :])

        if last:
            ktc, vc = _kv_projection(ctx, mod, ctx_row, w_in_b, k_gain, ones_bd)
        else:
            ypc, uc, m0c, qc, ktc, vc, agc = _in_projection(
                ctx, mod, ctx_row, w_in_b, q_gain, k_gain, ones_bd, None, *local_w)

        yp, u, m0, q, kt, v, ag = _in_projection(
            x, mod, lat_row, w_in_b, q_gain, k_gain, ones_bd, rope_tabs, *local_w)
        attn_o = _attention(q, ag, [(kt, v), (ktc, vc)])
        x_new = _out_projection(yp, hyena(u, m0, l), attn_o, x, mod, lat_row, w_out_b,
                                ln_g[i][None, :], ln_b[i][None, :], alpha)
        if not last:
            attn_c = _attention(qc, agc, [(ktc, vc)])
            ctx = _out_projection(ypc, hyena(uc, m0c, lc), attn_c, ctx, mod, ctx_row, w_out_b,
                                  ln_g[i][None, :], ln_b[i][None, :], alpha)
        x = x_new
    return x
```
